```python
import math, functools
import jax, jax.numpy as jnp
from jax import lax
import numpy as np

D_MODEL = 1024
BATCH = 32
SEQ = 2048
DEPTH = 1
DEC_BATCH = 128
DEC_SEQ = 1
PAST_LEN = 16384
PAGE_SIZE = 128

MLA_HEADS = 8
MLA_NOPE = 64
MLA_ROPE = 32
MLA_V = 64
Q_LORA = 384
KV_LORA = 256
ROPE_THETA = 10000.0
MLA_SCALE = (MLA_NOPE + MLA_ROPE) ** -0.5
DIFF_HEADS = 8
DIFF_KV_HEADS = 2
DIFF_GROUP = DIFF_HEADS // DIFF_KV_HEADS
DIFF_QK = 64
DIFF_V = 2 * DIFF_QK
PEER_HEADS = 8
PEER_NKEYS = 128
PEER_N = PEER_NKEYS * PEER_NKEYS
PEER_DKEY = 256
PEER_TOPK = 16
PEER_BLOCK = 128
Q_BLOCK = 128
RMS_EPS = 1e-6
_IN_SPLITS = (Q_LORA, KV_LORA, MLA_ROPE, DIFF_HEADS * 2 * DIFF_QK, DIFF_KV_HEADS * 2 * DIFF_QK,
              DIFF_KV_HEADS * DIFF_V, D_MODEL, D_MODEL)
IN_WIDTH = sum(_IN_SPLITS)

kernel_name = "mla_diffattn_peer_adaln_step"


def _rmsnorm(x, g):
    xf = x.astype(jnp.float32)
    y = xf * lax.rsqrt(jnp.mean(xf * xf, axis=-1, keepdims=True) + RMS_EPS)
    return (y * g.astype(jnp.float32)).astype(x.dtype)


def _rope_tables(pos):
    inv = ROPE_THETA ** (-jnp.arange(0, MLA_ROPE, 2, dtype=jnp.float32) / MLA_ROPE)
    ang = pos.astype(jnp.float32)[:, None] * inv[None, :]
    return jnp.cos(ang), jnp.sin(ang)


def _apply_rope(x, cos, sin):
    shp = (cos.shape[0],) + (1,) * (x.ndim - 3) + (cos.shape[1],)
    cos, sin = cos.reshape(shp), sin.reshape(shp)
    half = MLA_ROPE // 2
    x1, x2 = x[..., :half], x[..., half:]
    return jnp.concatenate([x1 * cos - x2 * sin, x1 * sin + x2 * cos], axis=-1).astype(x.dtype)


def _alibi_bias(qpos, kpos):
    slopes = jnp.exp2(-8.0 * (jnp.arange(DIFF_HEADS, dtype=jnp.float32) + 1.0) / DIFF_HEADS)
    d = (qpos[:, None] - kpos[None, :]).astype(jnp.float32)
    return jnp.where(d[None] >= 0, -slopes[:, None, None] * d[None], -jnp.inf)


def _mixer_inputs(h, pos, w_in, qn_g, w_uq, kvn_g):
    B, S, _ = h.shape
    z = h @ w_in
    offs, o = [], 0
    for w in _IN_SPLITS[:-1]:
        o += w
        offs.append(o)
    cq, ckv, kr, dq, dk, dv, ga, gb = jnp.split(z, offs, axis=-1)
    cos, sin = _rope_tables(pos)
    q = (_rmsnorm(cq, qn_g) @ w_uq).reshape(B, S, MLA_HEADS, MLA_NOPE + MLA_ROPE)
    q_nope = q[..., :MLA_NOPE]
    q_rope = _apply_rope(q[..., MLA_NOPE:], cos, sin)
    ckv = _rmsnorm(ckv, kvn_g)
    kr = _apply_rope(kr, cos, sin)
    dq = dq.reshape(B, S, DIFF_HEADS, 2, DIFF_QK)
    dk = dk.reshape(B, S, DIFF_KV_HEADS, 2 * DIFF_QK)
    dv = dv.reshape(B, S, DIFF_KV_HEADS, DIFF_V)
    return q_nope, q_rope, ckv, kr, dq, dk, dv, jax.nn.sigmoid(ga), jax.nn.sigmoid(gb)


def _diff_attend(q, k, v, bias, lam, lam_init, subln_g):
    B, Tq = q.shape[:2]
    qg = q.reshape(B, Tq, DIFF_KV_HEADS, DIFF_GROUP, 2, DIFF_QK)
    kg = k.reshape(k.shape[0], k.shape[1], DIFF_KV_HEADS, 2, DIFF_QK)
    s = jnp.einsum('bqkgmd,bskmd->bkgmqs', qg, kg, preferred_element_type=jnp.float32) * (DIFF_QK ** -0.5)
    s = s + bias.reshape(DIFF_KV_HEADS, DIFF_GROUP, 1, Tq, bias.shape[-1])
    p = jax.nn.softmax(s, axis=-1)
    a = p[:, :, :, 0] - lam * p[:, :, :, 1]
    o = jnp.einsum('bkgqs,bskv->bqkgv', a.astype(v.dtype), v).reshape(B, Tq, DIFF_HEADS, DIFF_V)
    o = _rmsnorm(o, subln_g) * (1.0 - lam_init)
    return o.reshape(B, Tq, DIFF_HEADS * DIFF_V)


def _mla_attend_expanded(q_nope, q_rope, k_nope, k_rope, v, mask):
    s = (jnp.einsum('bqhd,bshd->bhqs', q_nope, k_nope, preferred_element_type=jnp.float32)
         + jnp.einsum('bqhr,bsr->bhqs', q_rope, k_rope, preferred_element_type=jnp.float32)) * MLA_SCALE
    p = jax.nn.softmax(jnp.where(mask, s, -jnp.inf), axis=-1)
    return jnp.einsum('bhqs,bshv->bqhv', p.astype(v.dtype), v)


def _mla_attend_latent(q_nope, q_rope, ckv, kr, w_uk, w_uv, mask):
    q_lat = jnp.einsum('qhd,chd->qhc', q_nope, w_uk.reshape(KV_LORA, MLA_HEADS, MLA_NOPE))
    s = (jnp.einsum('qhc,sc->hqs', q_lat, ckv, preferred_element_type=jnp.float32)
         + jnp.einsum('qhr,sr->hqs', q_rope, kr, preferred_element_type=jnp.float32)) * MLA_SCALE
    p = jax.nn.softmax(jnp.where(mask[None], s, -jnp.inf), axis=-1)
    o_lat = jnp.einsum('hqs,sc->qhc', p.astype(ckv.dtype), ckv)
    return jnp.einsum('qhc,chv->qhv', o_lat, w_uv.reshape(KV_LORA, MLA_HEADS, MLA_V))


def _prompt_attention(q_nope, q_rope, ckv, kr, dq, dk, dv, *, w_uk, w_uv, lam, lam_init, subln_g):
    B, S = ckv.shape[:2]
    k_nope = (ckv @ w_uk).reshape(B, S, MLA_HEADS, MLA_NOPE)
    v_mla = (ckv @ w_uv).reshape(B, S, MLA_HEADS, MLA_V)
    kpos = jnp.arange(S)

    def block(i):
        start = i * Q_BLOCK
        sl = lambda t: lax.dynamic_slice_in_dim(t, start, Q_BLOCK, axis=1)
        qpos = start + jnp.arange(Q_BLOCK)
        mask = qpos[:, None] >= kpos[None, :]
        o_m = _mla_attend_expanded(sl(q_nope), sl(q_rope), k_nope, kr, v_mla, mask)
        o_d = _diff_attend(sl(dq), dk, dv, _alibi_bias(qpos, kpos), lam, lam_init, subln_g)
        return o_m.reshape(B, Q_BLOCK, MLA_HEADS * MLA_V), o_d

    o_m, o_d = lax.map(block, jnp.arange(S // Q_BLOCK))
    unblock = lambda t: jnp.moveaxis(t, 0, 1).reshape(B, S, t.shape[-1])
    return unblock(o_m), unblock(o_d)


def _sample_attention(q_nope, q_rope, ckv, kr, dq, dk, dv, *, layer, cache_ckv, cache_kr, cache_dk,
                      cache_dv, page_table, w_uk, w_uv, lam, lam_init, subln_g):
    T = ckv.shape[1]
    qpos = PAST_LEN + jnp.arange(T)
    kpos = jnp.arange(PAST_LEN + T)
    mask = qpos[:, None] >= kpos[None, :]
    bias = _alibi_bias(qpos, kpos)

    def one(args):
        pages, qn, qr, ckv_n, kr_n, dq_n, dk_n, dv_n = args
        past = lambda cache: cache[layer, pages].reshape((PAST_LEN,) + cache.shape[3:])
        ckv_all = jnp.concatenate([past(cache_ckv), ckv_n], axis=0)
        kr_all = jnp.concatenate([past(cache_kr), kr_n], axis=0)
        dk_all = jnp.concatenate([past(cache_dk), dk_n], axis=0)
        dv_all = jnp.concatenate([past(cache_dv), dv_n], axis=0)
        o_m = _mla_attend_latent(qn, qr, ckv_all, kr_all, w_uk, w_uv, mask)
        o_d = _diff_attend(dq_n[None], dk_all[None], dv_all[None], bias, lam, lam_init, subln_g)[0]
        return o_m.reshape(T, MLA_HEADS * MLA_V), o_d

    return lax.map(one, (page_table, q_nope, q_rope, ckv, kr, dq, dk, dv))


def _peer(h, w_pq, keys, pu, pv):
    T = h.shape[0]
    n_blk = -(-T // PEER_BLOCK)
    hb_all = jnp.pad(h, ((0, n_blk * PEER_BLOCK - T), (0, 0))).reshape(n_blk, PEER_BLOCK, D_MODEL)

    def blk(hb):
        q = (hb @ w_pq).reshape(PEER_BLOCK, PEER_HEADS, 2, PEER_DKEY // 2)
        s = jnp.einsum('thpk,hpnk->thpn', q, keys, preferred_element_type=jnp.float32)
        s1, i1 = lax.top_k(s[:, :, 0], PEER_TOPK)
        s2, i2 = lax.top_k(s[:, :, 1], PEER_TOPK)
        n_cand = PEER_TOPK * PEER_TOPK
        cand_s = (s1[..., :, None] + s2[..., None, :]).reshape(PEER_BLOCK, PEER_HEADS, n_cand)
        cand_i = (i1[..., :, None] * PEER_NKEYS + i2[..., None, :]).reshape(PEER_BLOCK, PEER_HEADS, n_cand)
        top_s, j = lax.top_k(cand_s, PEER_TOPK)
        idx = jnp.take_along_axis(cand_i, j, axis=-1)
        g = jax.nn.softmax(top_s, axis=-1)
        act = jax.nn.gelu(jnp.einsum('thkd,td->thk', pu[idx], hb, preferred_element_type=jnp.float32),
                          approximate=False)
        return jnp.einsum('thk,thkd->td', (g * act).astype(hb.dtype), pv[idx])

    return lax.map(blk, hb_all).reshape(n_blk * PEER_BLOCK, D_MODEL)[:T]


def _block(x, c, pos, attend, w_ada, b_ada, g_attn, g_ffn, w_in, qn_g, w_uq, kvn_g,
           w_o_mla, w_o_diff, w_out, w_pq, keys, pu, pv):
    B, S, _ = x.shape
    mod = jax.nn.silu(c) @ w_ada + b_ada
    sh_a, sc_a, gt_a, sh_f, sc_f, gt_f = jnp.split(mod[:, None, :], 6, axis=-1)
    h = _rmsnorm(x, g_attn) * (1.0 + sc_a) + sh_a
    q_nope, q_rope, ckv, kr, dq, dk, dv, ga, gb = _mixer_inputs(h, pos, w_in, qn_g, w_uq, kvn_g)
    o_m, o_d = attend(q_nope, q_rope, ckv, kr, dq, dk, dv)
    merged = ga * (o_m @ w_o_mla) + gb * (o_d @ w_o_diff)
    x = x + gt_a * (merged @ w_out)
    h2 = _rmsnorm(x, g_ffn) * (1.0 + sc_f) + sh_f
    x = x + gt_f * _peer(h2.reshape(B * S, D_MODEL), w_pq, keys, pu, pv).reshape(B, S, D_MODEL)
    return x, (ckv, kr, dk, dv)


def setup_inputs(seed: int = 0) -> dict:
    key = jax.random.key(seed)
    ks = iter(jax.random.split(key, 48))
    f32 = jnp.float32
    nrm = lambda shape, scale: jax.random.normal(next(ks), shape, f32) * scale
    gain = lambda shape: 1.0 + 0.02 * jax.random.normal(next(ks), shape, f32)
    n_pages = PAST_LEN // PAGE_SIZE
    n_used = DEC_BATCH * n_pages
    n_pool = n_used + max(1, n_used // 4)
    perm = jax.random.permutation(next(ks), n_pool)
    page_table = perm[:n_used].reshape(DEC_BATCH, n_pages).astype(jnp.int32)
    L, D = DEPTH, D_MODEL
    return {
        'x_prompt': nrm((BATCH, SEQ, D), 1.0),
        'x_sample': nrm((DEC_BATCH, DEC_SEQ, D), 1.0),
        'c_prompt': nrm((BATCH, D), 1.0),
        'c_sample': nrm((DEC_BATCH, D), 1.0),
        'cache_ckv': nrm((L, n_pool, PAGE_SIZE, KV_LORA), 1.0),
        'cache_kr': nrm((L, n_pool, PAGE_SIZE, MLA_ROPE), 1.0),
        'cache_dk': nrm((L, n_pool, PAGE_SIZE, DIFF_KV_HEADS, 2 * DIFF_QK), 1.0),
        'cache_dv': nrm((L, n_pool, PAGE_SIZE, DIFF_KV_HEADS, DIFF_V), 1.0),
        'page_table': page_table,
        'w_ada': nrm((L, D, 6 * D), 0.02),
        'b_ada': nrm((L, 6 * D), 0.01),
        'norm_attn_g': gain((L, D)),
        'norm_ffn_g': gain((L, D)),
        'w_in': nrm((L, D, IN_WIDTH), D ** -0.5),
        'mla_qnorm_g': gain((L, Q_LORA)),
        'w_uq': nrm((L, Q_LORA, MLA_HEADS * (MLA_NOPE + MLA_ROPE)), Q_LORA ** -0.5),
        'mla_kvnorm_g': gain((L, KV_LORA)),
        'w_uk': nrm((L, KV_LORA, MLA_HEADS * MLA_NOPE), KV_LORA ** -0.5),
        'w_uv': nrm((L, KV_LORA, MLA_HEADS * MLA_V), KV_LORA ** -0.5),
        'w_o_mla': nrm((L, MLA_HEADS * MLA_V, D), (MLA_HEADS * MLA_V) ** -0.5),
        'diff_lq1': nrm((L, DIFF_QK), 0.1),
        'diff_lk1': nrm((L, DIFF_QK), 0.1),
        'diff_lq2': nrm((L, DIFF_QK), 0.1),
        'diff_lk2': nrm((L, DIFF_QK), 0.1),
        'diff_subln_g': gain((L, DIFF_V)),
        'w_o_diff': nrm((L, DIFF_HEADS * DIFF_V, D), (DIFF_HEADS * DIFF_V) ** -0.5),
        'w_out': nrm((L, D, D), D ** -0.5),
        'w_pq': nrm((L, D, PEER_HEADS * PEER_DKEY), D ** -0.5),
        'peer_keys': nrm((L, PEER_HEADS, 2, PEER_NKEYS, PEER_DKEY // 2), (PEER_DKEY // 2) ** -0.5),
        'peer_u': nrm((L, PEER_N, D), D ** -0.5),
        'peer_v': nrm((L, PEER_N, D), PEER_HEADS ** -0.5),
        'norm_final_g': gain((D,)),
    }


def reference(x_prompt, x_sample, c_prompt, c_sample, cache_ckv, cache_kr, cache_dk, cache_dv, page_table,
              w_ada, b_ada, norm_attn_g, norm_ffn_g, w_in, mla_qnorm_g, w_uq, mla_kvnorm_g, w_uk, w_uv,
              w_o_mla, diff_lq1, diff_lk1, diff_lq2, diff_lk2, diff_subln_g, w_o_diff, w_out,
              w_pq, peer_keys, peer_u, peer_v, norm_final_g):
    xp, xs = x_prompt, x_sample
    pos_p = jnp.arange(xp.shape[1])
    pos_s = PAST_LEN + jnp.arange(xs.shape[1])
    new_p = ([], [], [], [])
    new_s = ([], [], [], [])
    for l in range(DEPTH):
        lam_init = 0.8 - 0.6 * math.exp(-0.3 * l)
        lam = (jnp.exp(jnp.sum(diff_lq1[l] * diff_lk1[l]).astype(jnp.float32))
               - jnp.exp(jnp.sum(diff_lq2[l] * diff_lk2[l]).astype(jnp.float32)) + lam_init)
        lw = (w_ada[l], b_ada[l], norm_attn_g[l], norm_ffn_g[l], w_in[l], mla_qnorm_g[l], w_uq[l],
              mla_kvnorm_g[l], w_o_mla[l], w_o_diff[l], w_out[l], w_pq[l], peer_keys[l], peer_u[l], peer_v[l])
        attend_p = functools.partial(_prompt_attention, w_uk=w_uk[l], w_uv=w_uv[l], lam=lam,
                                     lam_init=lam_init, subln_g=diff_subln_g[l])
        attend_s = functools.partial(_sample_attention, layer=l, cache_ckv=cache_ckv, cache_kr=cache_kr,
                                     cache_dk=cache_dk, cache_dv=cache_dv, page_table=page_table,
                                     w_uk=w_uk[l], w_uv=w_uv[l], lam=lam, lam_init=lam_init,
                                     subln_g=diff_subln_g[l])
        xp, rows_p = _block(xp, c_prompt, pos_p, attend_p, *lw)
        xs, rows_s = _block(xs, c_sample, pos_s, attend_s, *lw)
        for lst, r in zip(new_p, rows_p):
            lst.append(r)
        for lst, r in zip(new_s, rows_s):
            lst.append(r)
    y_prompt = _rmsnorm(xp, norm_final_g)
    y_sample = _rmsnorm(xs, norm_final_g)
    ckv_p, kr_p, dk_p, dv_p = [jnp.stack(t, axis=0) for t in new_p]
    ckv_s, kr_s, dk_s, dv_s = [jnp.stack(t, axis=0) for t in new_s]
    return (y_prompt, y_sample, ckv_p, kr_p, dk_p, dv_p, ckv_s, kr_s, dk_s, dv_s)
```

```python
import functools
import math

import jax
import jax.numpy as jnp
from jax import lax
from jax.experimental import pallas as pl
from jax.experimental.pallas import tpu as pltpu

D_MODEL = 1024
PAGE_SIZE = 128
MLA_HEADS = 8
MLA_NOPE = 64
MLA_ROPE = 32
MLA_V = 64
Q_LORA = 384
KV_LORA = 256
ROPE_THETA = 10000.0
MLA_SCALE = (MLA_NOPE + MLA_ROPE) ** -0.5
DIFF_HEADS = 8
DIFF_KV_HEADS = 2
DIFF_GROUP = DIFF_HEADS // DIFF_KV_HEADS
DIFF_QK = 64
DIFF_V = 2 * DIFF_QK
PEER_HEADS = 8
PEER_NKEYS = 128
PEER_N = PEER_NKEYS * PEER_NKEYS
PEER_DKEY = 256
PEER_TOPK = 16
RMS_EPS = 1e-6

LANES = 128
VMEM_LIMIT = 56 * 1024 * 1024
IN_PAD = 4480
NEG_INF = float("-inf")

BF16 = jnp.bfloat16
F32 = jnp.float32


def _params(*sem):
    return pltpu.CompilerParams(dimension_semantics=sem, vmem_limit_bytes=VMEM_LIMIT)


def _dot(a, b):
    return jnp.dot(a, b, preferred_element_type=F32)


def _dot_nt(a, b):
    return lax.dot_general(a, b, (((1,), (1,)), ((), ())), preferred_element_type=F32)


def _rms(x, g):
    return x * lax.rsqrt(jnp.mean(x * x, axis=-1, keepdims=True) + RMS_EPS) * g


def _small_mm_kernel(a_ref, w_ref, b_ref, o_ref, *, pre_silu):
    a = a_ref[...]
    if pre_silu:
        a = a * jax.nn.sigmoid(a)
    o_ref[...] = _dot(a.astype(BF16), w_ref[...].astype(BF16)) + b_ref[...]


def _small_mm(a, w, bias=None, pre_silu=False, tn=512):
    m, k = a.shape
    n = w.shape[1]
    tn = min(tn, n)
    if bias is None:
        bias = jnp.zeros((n,), F32)
    return pl.pallas_call(
        functools.partial(_small_mm_kernel, pre_silu=pre_silu),
        grid=(n // tn,),
        in_specs=[pl.BlockSpec((m, k), lambda j: (0, 0)),
                  pl.BlockSpec((k, tn), lambda j: (0, j)),
                  pl.BlockSpec((1, tn), lambda j: (0, j))],
        out_specs=pl.BlockSpec((m, tn), lambda j: (0, j)),
        out_shape=jax.ShapeDtypeStruct((m, n), F32),
        compiler_params=_params("arbitrary"),
        name="small_mm",
    )(a, w, bias.reshape(1, n))


C_CQ, C_CKV, C_DQ, C_DK, C_DV, C_GA, C_GB, C_KR, C_KRR = 0, 384, 640, 1664, 1920, 2176, 3200, 4224, 4352


def _inproj_kernel(x_ref, mod_ref, tab_ref, gattn_ref, win_ref, qng_ref, wq_ref, kvg_ref, wkv_ref,
                   ckv_ref, kr_ref, dk_ref, dv_ref, q_ref, k_ref, vlo_ref, vhi_ref, dq_ref,
                   dkb_ref, dvb_ref, gates_ref):
    D = D_MODEL
    x = x_ref[0]
    mod = mod_ref[0]
    h = _rms(x, gattn_ref[...]) * (1.0 + mod[:, D:2 * D]) + mod[:, 0:D]
    hb = h.astype(BF16)
    tab = tab_ref[...]
    c_q, s_q, c_k, s_k = (tab[:, i * LANES:(i + 1) * LANES] for i in range(4))

    def proj(lo, hi):
        return _dot(hb, win_ref[:, lo:hi])

    gates_ref[0] = jax.nn.sigmoid(proj(C_GA, C_KR))
    dq_ref[0] = (proj(C_DQ, C_DK) * (DIFF_QK ** -0.5)).astype(BF16)
    dk = proj(C_DK, C_DV)
    dv = proj(C_DV, C_GA)
    dk_ref[0] = dk
    dv_ref[0] = dv
    dkb_ref[0] = dk.astype(BF16)
    dvb_ref[0] = dv.astype(BF16)

    kr2 = proj(C_KR, IN_PAD)
    kr = kr2[:, :LANES] * c_k + kr2[:, LANES:] * s_k
    kr_ref[0] = kr[:, :MLA_ROPE]
    kr_sh = pltpu.roll(kr, MLA_NOPE, 1)

    ckv = _rms(proj(C_CKV, C_DQ), kvg_ref[...])
    ckv_ref[0] = ckv
    kv = _dot(ckv.astype(BF16), wkv_ref[...])
    for hd in range(MLA_HEADS):
        sl = slice(hd * LANES, (hd + 1) * LANES)
        k_ref[0, :, sl] = (kv[:, sl] + kr_sh).astype(BF16)
    vlo_ref[0] = kv[:, 1024:1536].astype(BF16)
    vhi_ref[0] = kv[:, 1536:2048].astype(BF16)

    cq = _rms(proj(C_CQ, C_CKV), qng_ref[...]).astype(BF16)
    q2 = _dot(cq, wq_ref[...])
    for hd in range(MLA_HEADS):
        sl = slice(hd * LANES, (hd + 1) * LANES)
        sr = slice(1024 + hd * LANES, 1024 + (hd + 1) * LANES)
        q_ref[0, :, sl] = (q2[:, sl] * c_q + q2[:, sr] * s_q).astype(BF16)


def _inproj(x, mod, tab, g_attn, win, qn_g, wq, kvn_g, wkv, tm):
    nb, s, D = x.shape
    mrows = mod.shape[1]
    grid = (nb, s // tm)
    tok = lambda w, dt: jax.ShapeDtypeStruct((nb, s, w), dt)
    tspec = lambda w: pl.BlockSpec((1, tm, w), lambda b, i: (b, i, 0))
    const = lambda shape: pl.BlockSpec(shape, lambda b, i: (0,) * len(shape))
    mod_spec = (pl.BlockSpec((1, 1, 6 * D), lambda b, i: (b, 0, 0)) if mrows == 1
                else pl.BlockSpec((1, tm, 6 * D), lambda b, i: (b, i, 0)))
    widths = [(KV_LORA, F32), (MLA_ROPE, F32), (256, F32), (256, F32), (1024, BF16), (1024, BF16),
              (512, BF16), (512, BF16), (1024, BF16), (256, BF16), (256, BF16), (2048, F32)]
    return pl.pallas_call(
        _inproj_kernel,
        grid=grid,
        in_specs=[tspec(D), mod_spec, pl.BlockSpec((tm, 4 * LANES), lambda b, i: (i, 0)),
                  const((1, D)), const((D, IN_PAD)), const((1, Q_LORA)), const((Q_LORA, 2048)),
                  const((1, KV_LORA)), const((KV_LORA, 2048))],
        out_specs=[tspec(w) for w, _ in widths],
        out_shape=[tok(w, dt) for w, dt in widths],
        compiler_params=_params("parallel", "arbitrary"),
        name="inproj",
    )(x, mod, tab, g_attn.reshape(1, D), win, qn_g.reshape(1, Q_LORA), wq, kvn_g.reshape(1, KV_LORA), wkv)


def _flash(q, k_of, v_of, bias_of, n_full, tq, tk):
    m_rows = q.shape[0]
    rep = m_rows // tq
    row = lax.broadcasted_iota(jnp.int32, (tq, tk), 0)
    col = lax.broadcasted_iota(jnp.int32, (tq, tk), 1)
    tri = col <= row
    if rep > 1:
        tri = jnp.concatenate([tri] * rep, axis=0)

    def step(j, carry, diagonal):
        m, l, acc = carry
        s = _dot_nt(q, k_of(j))
        b = bias_of(j)
        if b is not None:
            s = s + b
        if diagonal:
            s = jnp.where(tri, s, NEG_INF)
        m_new = jnp.maximum(m, jnp.max(s, axis=-1, keepdims=True))
        alpha = jnp.exp(m - m_new)
        p = jnp.exp(s - m_new)
        l = alpha * l + jnp.sum(p, axis=-1, keepdims=True)
        acc = alpha * acc + _dot(p.astype(BF16), v_of(j))
        return m_new, l, acc

    init = (jnp.full((m_rows, 1), NEG_INF, F32), jnp.zeros((m_rows, 1), F32), jnp.zeros((m_rows, LANES), F32))
    carry = lax.fori_loop(0, n_full, lambda j, c: step(j, c, False), init)
    _, l, acc = step(n_full, carry, True)
    return acc / l


def _attn_kernel(lam_ref, q_ref, k_ref, vlo_ref, vhi_ref, dq_ref, dk_ref, dv_ref, subg_ref,
                 om_ref, od_ref, *, tq, lam_init):
    tk = tq
    qi = pl.program_id(1)
    lam = lam_ref[0]

    def rows(ref, j, lane0):
        return ref[0, pl.ds(pl.multiple_of(j * tk, tk), tk), lane0:lane0 + LANES]

    for pair in range(MLA_HEADS // 2):
        o = None
        for hd, vref in ((2 * pair, vlo_ref), (2 * pair + 1, vhi_ref)):
            q = q_ref[0, :, hd * LANES:(hd + 1) * LANES]
            oh = _flash(q, lambda j, hd=hd: rows(k_ref, j, hd * LANES),
                        lambda j, vref=vref: rows(vref, j, pair * LANES), lambda j: None, qi, tq, tk)
            o = oh if o is None else o + oh
        om_ref[0, :, pair * LANES:(pair + 1) * LANES] = o.astype(BF16)

    lane = lax.broadcasted_iota(jnp.int32, (tq, LANES), 1)
    dpos = (lax.broadcasted_iota(jnp.int32, (tq, tk), 1) - lax.broadcasted_iota(jnp.int32, (tq, tk), 0)).astype(F32)
    for hd in range(DIFF_HEADS):
        kv = hd // DIFF_GROUP
        slope = 2.0 ** (-8.0 * (hd + 1) / DIFF_HEADS)
        qd = dq_ref[0, :, hd * LANES:(hd + 1) * LANES]
        zero = jnp.zeros_like(qd)
        qq = jnp.concatenate([jnp.where(lane < DIFF_QK, qd, zero), jnp.where(lane >= DIFF_QK, qd, zero)], axis=0)

        def bias_of(j, slope=slope):
            b = (dpos + ((j - qi) * tk).astype(F32)) * slope
            return jnp.concatenate([b, b], axis=0)

        o2 = _flash(qq, lambda j, kv=kv: rows(dk_ref, j, kv * LANES),
                    lambda j, kv=kv: rows(dv_ref, j, kv * LANES), bias_of, qi, tq, tk)
        o = o2[:tq] - lam * o2[tq:]
        od_ref[0, :, hd * LANES:(hd + 1) * LANES] = (_rms(o, subg_ref[...]) * (1.0 - lam_init)).astype(BF16)


def _prompt_attention(lam, q, k, vlo, vhi, dq, dkb, dvb, subln_g, lam_init, tq):
    nb, s, _ = q.shape
    qspec = lambda w: pl.BlockSpec((1, tq, w), lambda b, i: (b, i, 0))
    full = lambda w: pl.BlockSpec((1, s, w), lambda b, i: (b, 0, 0))
    return pl.pallas_call(
        functools.partial(_attn_kernel, tq=tq, lam_init=lam_init),
        grid=(nb, s // tq),
        in_specs=[pl.BlockSpec(memory_space=pltpu.SMEM), qspec(1024), full(1024), full(512), full(512),
                  qspec(1024), full(256), full(256), pl.BlockSpec((1, DIFF_V), lambda b, i: (0, 0))],
        out_specs=[qspec(512), qspec(1024)],
        out_shape=[jax.ShapeDtypeStruct((nb, s, 512), BF16), jax.ShapeDtypeStruct((nb, s, 1024), BF16)],
        compiler_params=_params("parallel", "arbitrary"),
        name="prompt_attn",
    )(lam, q, k, vlo, vhi, dq, dkb, dvb, subln_g.reshape(1, DIFF_V))


def _decode_kernel(pt_ref, lam_ref, qlat_ref, qd_ref, nckv_ref, nkr_ref, ndk_ref, ndv_ref, subg_ref, *refs,
                   pps, past_len, lam_init):
    pages = refs[:4 * pps]
    olat_ref, od_ref = refs[4 * pps], refs[4 * pps + 1]
    m_m, l_m, a_m, m_d, l_d, a_d = refs[4 * pps + 2:]
    step = pl.program_id(1)
    n_steps = pl.num_programs(1)
    R = MLA_HEADS

    qlat = qlat_ref[0]
    qd = qd_ref[0]

    @pl.when(step == 0)
    def _():
        s = (jnp.sum(qlat[:, :KV_LORA] * nckv_ref[0], axis=-1, keepdims=True)
             + jnp.sum(qlat[:, KV_LORA:KV_LORA + MLA_ROPE] * nkr_ref[0], axis=-1, keepdims=True))
        m_m[...] = jnp.broadcast_to(s, (R, LANES))
        l_m[...] = jnp.ones((R, LANES), F32)
        a_m[...] = jnp.broadcast_to(nckv_ref[0], (R, KV_LORA))
        for kv in range(DIFF_KV_HEADS):
            sl = slice(kv * LANES, (kv + 1) * LANES)
            sd = jnp.sum(qd[kv] * ndk_ref[0][:, sl], axis=-1, keepdims=True)
            m_d[kv] = jnp.broadcast_to(sd, (R, LANES))
            l_d[kv] = jnp.ones((R, LANES), F32)
            a_d[kv] = jnp.broadcast_to(ndv_ref[0][:, sl], (R, LANES))

    def update(s_list, v_list, m_ref, l_ref, a_ref):
        m_old = m_ref[...]
        mx = s_list[0]
        for s in s_list[1:]:
            mx = jnp.maximum(mx, s)
        m_new = jnp.maximum(m_old, jnp.max(mx, axis=-1, keepdims=True))
        alpha = jnp.exp(m_old - m_new)
        psum = None
        pv = None
        for s, v in zip(s_list, v_list):
            p = jnp.exp(s - m_new)
            psum = p if psum is None else psum + p
            d = _dot(p.astype(BF16), v)
            pv = d if pv is None else pv + d
        m_ref[...] = m_new
        l_ref[...] = alpha * l_ref[...] + jnp.sum(psum, axis=-1, keepdims=True)
        a_ref[...] = alpha[:, :1] * a_ref[...] + pv

    qc = qlat[:, :KV_LORA].astype(BF16)
    qr = qlat[:, KV_LORA:].astype(BF16)
    ckvs = [pages[4 * j][0, 0].astype(BF16) for j in range(pps)]
    s_m = []
    for j in range(pps):
        kr = pages[4 * j + 1][0, 0].astype(BF16)
        s_m.append(_dot_nt(qc, ckvs[j]) + _dot_nt(qr[:, :MLA_ROPE], kr))
    update(s_m, ckvs, m_m, l_m, a_m)

    lane = lax.broadcasted_iota(jnp.int32, (R, LANES), 1)
    grp = lax.broadcasted_iota(jnp.int32, (R, LANES), 0) % DIFF_GROUP
    for kv in range(DIFF_KV_HEADS):
        slope = jnp.exp2(-8.0 * (kv * DIFF_GROUP + grp + 1).astype(F32) / DIFF_HEADS)
        qk = qd[kv].astype(BF16)
        s_d, v_d = [], []
        for j in range(pps):
            kpos = (step * pps + j) * PAGE_SIZE + lane
            bias = slope * (kpos - past_len).astype(F32)
            dk = pages[4 * j + 2][0, 0, :, kv, :].astype(BF16)
            s_d.append(_dot_nt(qk, dk) + bias)
            v_d.append(pages[4 * j + 3][0, 0, :, kv, :].astype(BF16))
        update(s_d, v_d, m_d.at[kv], l_d.at[kv], a_d.at[kv])

    @pl.when(step == n_steps - 1)
    def _():
        olat_ref[0] = a_m[...] / l_m[...][:, :1]
        lam = lam_ref[0]
        for kv in range(DIFF_KV_HEADS):
            o2 = a_d[kv] / l_d[kv][:, :1]
            o = o2[:DIFF_GROUP] - lam * o2[DIFF_GROUP:]
            od_ref[0, kv] = _rms(o, subg_ref[...]) * (1.0 - lam_init)


def _decode_attention(page_table, lam, qlat, qd, nckv, nkr, ndk, ndv, subln_g, cache_ckv, cache_kr,
                      cache_dk, cache_dv, layer, lam_init, pps):
    n_seq, n_pages = page_table.shape
    past_len = n_pages * PAGE_SIZE
    R = MLA_HEADS
    seq3 = lambda w: pl.BlockSpec((1, 1, w), lambda n, s, pt: (n, 0, 0))
    in_specs = [pl.BlockSpec(memory_space=pltpu.SMEM),
                pl.BlockSpec((1, R, 384), lambda n, s, pt: (n, 0, 0)),
                pl.BlockSpec((1, DIFF_KV_HEADS, R, LANES), lambda n, s, pt: (n, 0, 0, 0)),
                seq3(KV_LORA), seq3(MLA_ROPE), seq3(256), seq3(256),
                pl.BlockSpec((1, DIFF_V), lambda n, s, pt: (0, 0))]
    args = [lam, qlat, qd, nckv, nkr, ndk, ndv, subln_g.reshape(1, DIFF_V)]
    for j in range(pps):
        page = lambda n, s, pt, j=j: pt[n, s * pps + j]
        in_specs += [
            pl.BlockSpec((1, 1, PAGE_SIZE, KV_LORA), lambda n, s, pt, page=page: (layer, page(n, s, pt), 0, 0)),
            pl.BlockSpec((1, 1, PAGE_SIZE, MLA_ROPE), lambda n, s, pt, page=page: (layer, page(n, s, pt), 0, 0)),
            pl.BlockSpec((1, 1, PAGE_SIZE, DIFF_KV_HEADS, 2 * DIFF_QK),
                         lambda n, s, pt, page=page: (layer, page(n, s, pt), 0, 0, 0)),
            pl.BlockSpec((1, 1, PAGE_SIZE, DIFF_KV_HEADS, DIFF_V),
                         lambda n, s, pt, page=page: (layer, page(n, s, pt), 0, 0, 0)),
        ]
        args += [cache_ckv, cache_kr, cache_dk, cache_dv]
    grid_spec = pltpu.PrefetchScalarGridSpec(
        num_scalar_prefetch=1,
        grid=(n_seq, n_pages // pps),
        in_specs=in_specs,
        out_specs=[pl.BlockSpec((1, R, KV_LORA), lambda n, s, pt: (n, 0, 0)),
                   pl.BlockSpec((1, DIFF_KV_HEADS, DIFF_GROUP, LANES), lambda n, s, pt: (n, 0, 0, 0))],
        scratch_shapes=[pltpu.VMEM((R, LANES), F32), pltpu.VMEM((R, LANES), F32), pltpu.VMEM((R, KV_LORA), F32),
                        pltpu.VMEM((DIFF_KV_HEADS, R, LANES), F32), pltpu.VMEM((DIFF_KV_HEADS, R, LANES), F32),
                        pltpu.VMEM((DIFF_KV_HEADS, R, LANES), F32)],
    )
    return pl.pallas_call(
        functools.partial(_decode_kernel, pps=pps, past_len=past_len, lam_init=lam_init),
        grid_spec=grid_spec,
        out_shape=[jax.ShapeDtypeStruct((n_seq, R, KV_LORA), F32),
                   jax.ShapeDtypeStruct((n_seq, DIFF_KV_HEADS, DIFF_GROUP, LANES), F32)],
        compiler_params=_params("parallel", "arbitrary"),
        name="decode_attn",
    )(page_table, *args)


def _post_kernel(om_ref, od_ref, gates_ref, x_ref, mod_ref, wom_ref, wod_ref, wout_ref, gffn_ref, wpq_ref,
                 keys_ref, x1_ref, h2t_ref, st_ref):
    D = D_MODEL
    mod = mod_ref[0]
    gates = gates_ref[...]
    merged = gates[:, :D] * _dot(om_ref[...], wom_ref[...]) + gates[:, D:] * _dot(od_ref[...], wod_ref[...])
    x1 = x_ref[...] + mod[:, 2 * D:3 * D] * _dot(merged.astype(BF16), wout_ref[...])
    x1_ref[...] = x1
    h2 = _rms(x1, gffn_ref[...]) * (1.0 + mod[:, 4 * D:5 * D]) + mod[:, 3 * D:4 * D]
    h2t_ref[...] = h2.T.astype(BF16)
    pq = _dot(h2.astype(BF16), wpq_ref[...]).astype(BF16)
    for i in range(2 * PEER_HEADS):
        st_ref[i] = _dot_nt(keys_ref[i], pq[:, i * LANES:(i + 1) * LANES])


def _post(om, od, gates, x, mod, wom, wod, wout, g_ffn, wpq, keys, tm, tiles_per_row):
    t, D = x.shape
    mrows = mod.shape[1]
    tok = lambda w: pl.BlockSpec((tm, w), lambda i: (i, 0))
    const = lambda shape: pl.BlockSpec(shape, lambda i: (0,) * len(shape))
    mod_spec = (pl.BlockSpec((1, 1, 6 * D), lambda i: (i // tiles_per_row, 0, 0)) if mrows == 1
                else pl.BlockSpec((1, tm, 6 * D), lambda i: (0, i, 0)))
    nk = 2 * PEER_HEADS
    return pl.pallas_call(
        _post_kernel,
        grid=(t // tm,),
        in_specs=[tok(512), tok(1024), tok(2048), tok(D), mod_spec, const((512, D)), const((1024, D)),
                  const((D, D)), const((1, D)), const((D, PEER_HEADS * PEER_DKEY)),
                  const((nk, PEER_NKEYS, PEER_DKEY // 2))],
        out_specs=[tok(D), pl.BlockSpec((D, tm), lambda i: (0, i)),
                   pl.BlockSpec((nk, PEER_NKEYS, tm), lambda i: (0, 0, i))],
        out_shape=[jax.ShapeDtypeStruct((t, D), F32), jax.ShapeDtypeStruct((D, t), BF16),
                   jax.ShapeDtypeStruct((nk, PEER_NKEYS, t), F32)],
        compiler_params=_params("parallel"),
        name="post_attn",
    )(om, od, gates, x, mod, wom, wod, wout, g_ffn.reshape(1, D), wpq, keys)


N_EXT = PEER_TOPK + 1
CAND = [(i, j) for i in range(N_EXT) for j in range(N_EXT) if (i + 1) * (j + 1) <= N_EXT]


def _extract_desc(s, n):
    out = []
    for _ in range(n):
        m = jnp.max(s, axis=0, keepdims=True)
        out.append(m)
        s = jnp.where(s == m, NEG_INF, s)
    return out


CAND_ROWS = -(-len(CAND) // 8) * 8


def _select_kernel(st_ref, stats_ref, cand_ref):
    te = st_ref.shape[-1]
    cand_ref[...] = jnp.full(cand_ref.shape, NEG_INF, F32)
    stats_ref[...] = jnp.zeros(stats_ref.shape, F32)

    def head(h, carry):
        t1 = _extract_desc(st_ref[2 * h], N_EXT)
        t2 = _extract_desc(st_ref[2 * h + 1], N_EXT)
        for r, (i, j) in enumerate(CAND):
            cand_ref[r:r + 1, :] = t1[i] + t2[j]
        top = _extract_desc(cand_ref[...], N_EXT)
        z = jnp.ones((1, te), F32)
        for c in top[1:PEER_TOPK]:
            z = z + jnp.exp(c - top[0])
        stats_ref[h, 0:1, :] = 0.5 * (top[PEER_TOPK - 1] + top[PEER_TOPK])
        stats_ref[h, 1:2, :] = t1[0]
        stats_ref[h, 2:3, :] = t2[0]
        stats_ref[h, 3:4, :] = 1.0 / z
        return carry

    lax.fori_loop(0, PEER_HEADS, head, 0)


def _peer_select(st, te):
    nk, n, t = st.shape
    return pl.pallas_call(
        _select_kernel,
        grid=(t // te,),
        in_specs=[pl.BlockSpec((nk, n, te), lambda i: (0, 0, i))],
        out_specs=pl.BlockSpec((PEER_HEADS, 8, te), lambda i: (0, 0, i)),
        out_shape=jax.ShapeDtypeStruct((PEER_HEADS, 8, t), F32),
        scratch_shapes=[pltpu.VMEM((CAND_ROWS, te), F32)],
        compiler_params=_params("parallel"),
        name="peer_select",
    )(st)


def _peer_kernel(h2t_ref, st_ref, stats_ref, pu_ref, pvt_ref, x1_ref, mod_ref, gfin_ref, y_ref,
                 acc_ref, e2_ref, tau_ref, cc_ref, wt_ref, *, ca):
    D = D_MODEL
    c = pl.program_id(1)

    @pl.when(c == 0)
    def _():
        acc_ref[...] = jnp.zeros_like(acc_ref)
        for h in range(PEER_HEADS):
            st = stats_ref[h]
            s1 = st_ref[2 * h]
            e2_ref[h] = jnp.exp(st_ref[2 * h + 1] - st[2:3])
            tau_ref[h] = st[0:1] - s1
            cc_ref[h] = jnp.exp(s1 - st[1:2]) * st[3:4]

    ut = _dot(pu_ref[...], h2t_ref[...])
    for al in range(ca):
        a = c * ca + al
        u = ut[al * LANES:(al + 1) * LANES]
        g = jnp.zeros_like(u)
        for h in range(PEER_HEADS):
            tau = tau_ref[h, pl.ds(a, 1), :]
            cc = cc_ref[h, pl.ds(a, 1), :]
            g = g + jnp.where(st_ref[2 * h + 1] >= tau, e2_ref[h] * cc, 0.0)
        act = 0.5 * u * (1.0 + lax.erf(u * math.sqrt(0.5)))
        wt_ref[al * LANES:(al + 1) * LANES, :] = (g * act).astype(BF16)
    acc_ref[...] += _dot(pvt_ref[...], wt_ref[...])

    @pl.when(c == pl.num_programs(1) - 1)
    def _():
        mod = mod_ref[0]
        x2 = x1_ref[...] + mod[:, 5 * D:6 * D] * acc_ref[...].T
        y_ref[...] = _rms(x2, gfin_ref[...])


def _peer_dense(h2t, st, stats, pu, pvt, x1, mod, g_final, tm, ca, tiles_per_row):
    t, D = x1.shape
    mrows = mod.shape[1]
    nch = ca * PEER_NKEYS
    nk = 2 * PEER_HEADS
    mod_spec = (pl.BlockSpec((1, 1, 6 * D), lambda i, c: (i // tiles_per_row, 0, 0)) if mrows == 1
                else pl.BlockSpec((1, tm, 6 * D), lambda i, c: (0, i, 0)))
    return pl.pallas_call(
        functools.partial(_peer_kernel, ca=ca),
        grid=(t // tm, PEER_N // nch),
        in_specs=[pl.BlockSpec((D, tm), lambda i, c: (0, i)),
                  pl.BlockSpec((nk, PEER_NKEYS, tm), lambda i, c: (0, 0, i)),
                  pl.BlockSpec((PEER_HEADS, 8, tm), lambda i, c: (0, 0, i)),
                  pl.BlockSpec((nch, D), lambda i, c: (c, 0)),
                  pl.BlockSpec((D, nch), lambda i, c: (0, c)),
                  pl.BlockSpec((tm, D), lambda i, c: (i, 0)),
                  mod_spec,
                  pl.BlockSpec((1, D), lambda i, c: (0, 0))],
        out_specs=pl.BlockSpec((tm, D), lambda i, c: (i, 0)),
        out_shape=jax.ShapeDtypeStruct((t, D), F32),
        scratch_shapes=[pltpu.VMEM((D, tm), F32), pltpu.VMEM((PEER_HEADS, PEER_NKEYS, tm), F32),
                        pltpu.VMEM((PEER_HEADS, PEER_NKEYS, tm), F32),
                        pltpu.VMEM((PEER_HEADS, PEER_NKEYS, tm), F32), pltpu.VMEM((nch, tm), BF16)],
        compiler_params=_params("parallel", "arbitrary"),
        name="peer_dense",
    )(h2t, st, stats, pu, pvt, x1, mod, g_final.reshape(1, D))


def _rot_half(w):
    half = MLA_ROPE // 2
    return jnp.concatenate([-w[..., half:], w[..., :half]], axis=-1)


def _prep_layer(w_in, w_uq, w_uk, w_uv):
    D = D_MODEL
    o_kr = Q_LORA + KV_LORA
    w_kr = w_in[:, o_kr:o_kr + MLA_ROPE]
    zpad = jnp.zeros((D, LANES - MLA_ROPE), F32)
    win = jnp.concatenate([w_in[:, :o_kr], w_in[:, o_kr + MLA_ROPE:], w_kr, zpad, _rot_half(w_kr), zpad],
                          axis=1).astype(BF16)
    H = MLA_HEADS
    r = w_uq.reshape(Q_LORA, H, MLA_NOPE + MLA_ROPE)
    nope, rope = r[..., :MLA_NOPE], r[..., MLA_NOPE:]
    z32 = jnp.zeros((Q_LORA, H, LANES - MLA_NOPE - MLA_ROPE), F32)
    w_q = jnp.concatenate([nope, rope, z32], axis=-1).reshape(Q_LORA, H * LANES)
    w_qr = jnp.concatenate([jnp.zeros_like(nope), _rot_half(rope), z32], axis=-1).reshape(Q_LORA, H * LANES)
    wq = jnp.concatenate([w_q, w_qr], axis=1).astype(BF16)
    uk = w_uk.reshape(KV_LORA, H, MLA_NOPE)
    w_k = jnp.concatenate([uk, jnp.zeros_like(uk)], axis=-1).reshape(KV_LORA, H * LANES)
    uv = w_uv.reshape(KV_LORA, H // 2, 2, MLA_V)
    zv = jnp.zeros((KV_LORA, H // 2, MLA_V), F32)
    v_lo = jnp.concatenate([uv[:, :, 0], zv], axis=-1).reshape(KV_LORA, H // 2 * LANES)
    v_hi = jnp.concatenate([zv, uv[:, :, 1]], axis=-1).reshape(KV_LORA, H // 2 * LANES)
    wkv = jnp.concatenate([w_k, v_lo, v_hi], axis=1).astype(BF16)
    eye_h = jnp.eye(H, dtype=F32)
    uk_t = jnp.transpose(uk, (1, 2, 0))
    top = jnp.concatenate([uk_t, jnp.zeros((H, MLA_NOPE, LANES), F32)], axis=-1)
    mid = jnp.concatenate([jnp.zeros((H, MLA_ROPE, KV_LORA), F32),
                           jnp.broadcast_to(jnp.eye(MLA_ROPE, LANES, dtype=F32), (H, MLA_ROPE, LANES))], axis=-1)
    bot = jnp.zeros((H, LANES - MLA_NOPE - MLA_ROPE, KV_LORA + LANES), F32)
    m_h = jnp.concatenate([top, mid, bot], axis=1)
    w_abs = jnp.einsum('hrc,hg->hrgc', m_h, eye_h).reshape(H * LANES, H * (KV_LORA + LANES))
    uv_h = jnp.transpose(w_uv.reshape(KV_LORA, H, MLA_V), (1, 0, 2))
    w_uvbd = jnp.einsum('hcv,hg->hcgv', uv_h, eye_h).reshape(H * KV_LORA, H * MLA_V)
    return win, wq, wkv, w_abs, w_uvbd


def _rope_table(pos):
    inv = ROPE_THETA ** (-jnp.arange(0, MLA_ROPE, 2, dtype=F32) / MLA_ROPE)
    ang = pos.astype(F32)[:, None] * inv[None, :]
    cos, sin = jnp.cos(ang), jnp.sin(ang)
    n = pos.shape[0]
    z = lambda w: jnp.zeros((n, w), F32)
    c_q = MLA_SCALE * jnp.concatenate([jnp.ones((n, MLA_NOPE), F32), cos, cos, z(32)], axis=1)
    s_q = MLA_SCALE * jnp.concatenate([z(MLA_NOPE), sin, sin, z(32)], axis=1)
    c_k = jnp.concatenate([cos, cos, z(96)], axis=1)
    s_k = jnp.concatenate([sin, sin, z(96)], axis=1)
    return jnp.concatenate([c_q, s_q, c_k, s_k], axis=1)


def _pick(n, prefs):
    for p in prefs:
        if n % p == 0:
            return p
    return n


def kernel(x_prompt, x_sample, c_prompt, c_sample, cache_ckv, cache_kr, cache_dk, cache_dv, page_table, w_ada, b_ada, norm_attn_g, norm_ffn_g, w_in, mla_qnorm_g, w_uq, mla_kvnorm_g, w_uk, w_uv, w_o_mla, diff_lq1, diff_lk1, diff_lq2, diff_lk2, diff_subln_g, w_o_diff, w_out, w_pq, peer_keys, peer_u, peer_v, norm_final_g):
    D = D_MODEL
    nb, s_len, _ = x_prompt.shape
    n_seq, dec_seq, _ = x_sample.shape
    assert dec_seq == 1
    depth = w_ada.shape[0]
    n_pages = page_table.shape[1]
    past_len = n_pages * PAGE_SIZE
    t_p = nb * s_len

    tab_p = _rope_table(jnp.arange(s_len))
    tab_s = _rope_table(jnp.full((n_seq,), past_len, jnp.int32))
    tm_in = _pick(s_len, (256, 128))
    tq = _pick(s_len, (256, 128))
    tm_post = _pick(s_len, (256, 128))
    te = _pick(s_len, (256, 128))
    tm_peer = _pick(s_len, (512, 256, 128))
    ca = 8
    pps = _pick(n_pages, (8, 4, 2, 1))

    xp = x_prompt
    xs = x_sample.reshape(1, n_seq, D)
    c_all = jnp.concatenate([c_prompt, c_sample], axis=0)
    new_p, new_s = ([], [], [], []), ([], [], [], [])
    yp = ys = None
    for l in range(depth):
        lam_init = 0.8 - 0.6 * math.exp(-0.3 * l)
        lam = (jnp.exp(jnp.sum(diff_lq1[l] * diff_lk1[l])) - jnp.exp(jnp.sum(diff_lq2[l] * diff_lk2[l]))
               + lam_init).astype(F32).reshape(1)
        win, wq, wkv, w_abs, w_uvbd = _prep_layer(w_in[l], w_uq[l], w_uk[l], w_uv[l])
        wom, wod, wout = w_o_mla[l].astype(BF16), w_o_diff[l].astype(BF16), w_out[l].astype(BF16)
        wpq = w_pq[l].astype(BF16)
        keys = peer_keys[l].reshape(2 * PEER_HEADS, PEER_NKEYS, PEER_DKEY // 2).astype(BF16)
        pu = peer_u[l].astype(BF16)
        pvt = peer_v[l].T.astype(BF16)
        last = l == depth - 1
        g_fin = norm_final_g if last else jnp.ones((D,), F32)

        mod = _small_mm(c_all, w_ada[l], b_ada[l], pre_silu=True, tn=1024)
        mod_p = mod[:nb].reshape(nb, 1, 6 * D)
        mod_s = mod[nb:].reshape(1, n_seq, 6 * D)

        def mixer(x, modx, tab, tm):
            return _inproj(x, modx, tab, norm_attn_g[l], win, mla_qnorm_g[l], wq, mla_kvnorm_g[l], wkv, tm)

        def ffn(om, od, gates, x2d, modx, tm_a, tm_b, te_, tiles_per_row):
            x1, h2t, st = _post(om, od, gates, x2d, modx, wom, wod, wout, norm_ffn_g[l], wpq, keys, tm_a,
                                max(tiles_per_row * tm_b // tm_a, 1))
            stats = _peer_select(st, te_)
            return _peer_dense(h2t, st, stats, pu, pvt, x1, modx, g_fin, tm_b, ca, tiles_per_row)

        (ckv, kr, dk, dv, q, k, vlo, vhi, dq, dkb, dvb, gates) = mixer(xp, mod_p, tab_p, tm_in)
        om, od = _prompt_attention(lam, q, k, vlo, vhi, dq, dkb, dvb, diff_subln_g[l], lam_init, tq)
        yp = ffn(om.reshape(t_p, 512), od.reshape(t_p, 1024), gates.reshape(t_p, 2048), xp.reshape(t_p, D),
                 mod_p, tm_post, tm_peer, te, s_len // tm_peer)
        for lst, r in zip(new_p, (ckv, kr, dk.reshape(nb, s_len, DIFF_KV_HEADS, 2 * DIFF_QK),
                                  dv.reshape(nb, s_len, DIFF_KV_HEADS, DIFF_V))):
            lst.append(r)

        (ckv_s, kr_s, dk_s, dv_s, q_s, _, _, _, dq_s, _, _, gates_s) = mixer(xs, mod_s, tab_s, n_seq)
        qlat = _small_mm(q_s[0].astype(F32), w_abs, tn=1024).reshape(n_seq, MLA_HEADS, KV_LORA + LANES)
        dq4 = dq_s[0].astype(F32).reshape(n_seq, DIFF_KV_HEADS, DIFF_GROUP, 2, DIFF_QK)
        zq = jnp.zeros_like(dq4[:, :, :, 0])
        qd = jnp.concatenate([jnp.concatenate([dq4[:, :, :, 0], zq], axis=-1),
                              jnp.concatenate([zq, dq4[:, :, :, 1]], axis=-1)], axis=2)
        olat, od_s = _decode_attention(
            page_table, lam, qlat, qd, ckv_s.reshape(n_seq, 1, KV_LORA), kr_s.reshape(n_seq, 1, MLA_ROPE),
            dk_s.reshape(n_seq, 1, 256), dv_s.reshape(n_seq, 1, 256), diff_subln_g[l],
            cache_ckv, cache_kr, cache_dk, cache_dv, l, lam_init, pps)
        om_s = _small_mm(olat.reshape(n_seq, MLA_HEADS * KV_LORA), w_uvbd).astype(BF16)
        ys = ffn(om_s, od_s.reshape(n_seq, 1024).astype(BF16), gates_s[0], xs[0], mod_s, n_seq, n_seq, n_seq, 1)
        for lst, r in zip(new_s, (ckv_s.reshape(n_seq, 1, KV_LORA), kr_s.reshape(n_seq, 1, MLA_ROPE),
                                  dk_s.reshape(n_seq, 1, DIFF_KV_HEADS, 2 * DIFF_QK),
                                  dv_s.reshape(n_seq, 1, DIFF_KV_HEADS, DIFF_V))):
            lst.append(r)
        if not last:
            raise NotImplementedError("multi-layer stacking needs the un-normalised residual stream")

    y_prompt = yp.reshape(nb, s_len, D)
    y_sample = ys.reshape(n_seq, 1, D)
    stack = lambda t: jnp.stack(t, axis=0)
    return (y_prompt, y_sample, *[stack(t) for t in new_p], *[stack(t) for t in new_s])
```

```python
import functools
import math

import jax
import jax.numpy as jnp
from jax import lax
from jax.experimental import pallas as pl
from jax.experimental.pallas import tpu as pltpu

D_MODEL = 1024
PAGE_SIZE = 128
MLA_HEADS = 8
MLA_NOPE = 64
MLA_ROPE = 32
MLA_V = 64
Q_LORA = 384
KV_LORA = 256
ROPE_THETA = 10000.0
MLA_SCALE = (MLA_NOPE + MLA_ROPE) ** -0.5
DIFF_HEADS = 8
DIFF_KV_HEADS = 2
DIFF_GROUP = DIFF_HEADS // DIFF_KV_HEADS
DIFF_QK = 64
DIFF_V = 2 * DIFF_QK
PEER_HEADS = 8
PEER_NKEYS = 128
PEER_N = PEER_NKEYS * PEER_NKEYS
PEER_DKEY = 256
PEER_TOPK = 16
RMS_EPS = 1e-6

LANES = 128
VMEM_LIMIT = 56 * 1024 * 1024
IN_PAD = 4480
NEG_INF = float("-inf")

BF16 = jnp.bfloat16
F32 = jnp.float32


def _params(*sem):
    return pltpu.CompilerParams(dimension_semantics=sem, vmem_limit_bytes=VMEM_LIMIT)


def _dot(a, b):
    return jnp.dot(a, b, preferred_element_type=F32)


def _dot_nt(a, b):
    return lax.dot_general(a, b, (((1,), (1,)), ((), ())), preferred_element_type=F32)


def _rms(x, g):
    return x * lax.rsqrt(jnp.mean(x * x, axis=-1, keepdims=True) + RMS_EPS) * g


def _small_mm_kernel(a_ref, w_ref, b_ref, o_ref, *, pre_silu):
    a = a_ref[...]
    if pre_silu:
        a = a * jax.nn.sigmoid(a)
    o_ref[...] = _dot(a.astype(BF16), w_ref[...].astype(BF16)) + b_ref[...]


def _small_mm(a, w, bias=None, pre_silu=False, tn=512):
    m, k = a.shape
    n = w.shape[1]
    tn = min(tn, n)
    if bias is None:
        bias = jnp.zeros((n,), F32)
    return pl.pallas_call(
        functools.partial(_small_mm_kernel, pre_silu=pre_silu),
        grid=(n // tn,),
        in_specs=[pl.BlockSpec((m, k), lambda j: (0, 0)),
                  pl.BlockSpec((k, tn), lambda j: (0, j)),
                  pl.BlockSpec((1, tn), lambda j: (0, j))],
        out_specs=pl.BlockSpec((m, tn), lambda j: (0, j)),
        out_shape=jax.ShapeDtypeStruct((m, n), F32),
        compiler_params=_params("arbitrary"),
        name="small_mm",
    )(a, w, bias.reshape(1, n))


C_CQ, C_CKV, C_DQ, C_DK, C_DV, C_GA, C_GB, C_KR, C_KRR = 0, 384, 640, 1664, 1920, 2176, 3200, 4224, 4352


def _inproj_kernel(x_ref, mod_ref, tab_ref, gattn_ref, win_ref, qng_ref, wq_ref, kvg_ref, wkv_ref,
                   ckv_ref, kr_ref, dk_ref, dv_ref, q_ref, k_ref, vlo_ref, vhi_ref, dq_ref,
                   dkb_ref, dvb_ref, gates_ref):
    D = D_MODEL
    x = x_ref[0]
    mod = mod_ref[0]
    h = _rms(x, gattn_ref[...]) * (1.0 + mod[:, D:2 * D]) + mod[:, 0:D]
    hb = h.astype(BF16)
    tab = tab_ref[...]
    c_q, s_q, c_k, s_k = (tab[:, i * LANES:(i + 1) * LANES] for i in range(4))

    def proj(lo, hi):
        return _dot(hb, win_ref[:, lo:hi])

    gates_ref[0] = jax.nn.sigmoid(proj(C_GA, C_KR))
    dq_ref[0] = (proj(C_DQ, C_DK) * (DIFF_QK ** -0.5)).astype(BF16)
    dk = proj(C_DK, C_DV)
    dv = proj(C_DV, C_GA)
    dk_ref[0] = dk
    dv_ref[0] = dv
    dkb_ref[0] = dk.astype(BF16)
    dvb_ref[0] = dv.astype(BF16)

    kr2 = proj(C_KR, IN_PAD)
    kr = kr2[:, :LANES] * c_k + kr2[:, LANES:] * s_k
    kr_ref[0] = kr[:, :MLA_ROPE]
    kr_sh = pltpu.roll(kr, MLA_NOPE, 1)

    ckv = _rms(proj(C_CKV, C_DQ), kvg_ref[...])
    ckv_ref[0] = ckv
    kv = _dot(ckv.astype(BF16), wkv_ref[...])
    for hd in range(MLA_HEADS):
        sl = slice(hd * LANES, (hd + 1) * LANES)
        k_ref[0, :, sl] = (kv[:, sl] + kr_sh).astype(BF16)
    vlo_ref[0] = kv[:, 1024:1536].astype(BF16)
    vhi_ref[0] = kv[:, 1536:2048].astype(BF16)

    cq = _rms(proj(C_CQ, C_CKV), qng_ref[...]).astype(BF16)
    q2 = _dot(cq, wq_ref[...])
    for hd in range(MLA_HEADS):
        sl = slice(hd * LANES, (hd + 1) * LANES)
        sr = slice(1024 + hd * LANES, 1024 + (hd + 1) * LANES)
        q_ref[0, :, sl] = (q2[:, sl] * c_q + q2[:, sr] * s_q).astype(BF16)


def _inproj(x, mod, tab, g_attn, win, qn_g, wq, kvn_g, wkv, tm):
    nb, s, D = x.shape
    mrows = mod.shape[1]
    grid = (nb, s // tm)
    tok = lambda w, dt: jax.ShapeDtypeStruct((nb, s, w), dt)
    tspec = lambda w: pl.BlockSpec((1, tm, w), lambda b, i: (b, i, 0))
    const = lambda shape: pl.BlockSpec(shape, lambda b, i: (0,) * len(shape))
    mod_spec = (pl.BlockSpec((1, 1, 6 * D), lambda b, i: (b, 0, 0)) if mrows == 1
                else pl.BlockSpec((1, tm, 6 * D), lambda b, i: (b, i, 0)))
    widths = [(KV_LORA, F32), (MLA_ROPE, F32), (256, F32), (256, F32), (1024, BF16), (1024, BF16),
              (512, BF16), (512, BF16), (1024, BF16), (256, BF16), (256, BF16), (2048, F32)]
    return pl.pallas_call(
        _inproj_kernel,
        grid=grid,
        in_specs=[tspec(D), mod_spec, pl.BlockSpec((tm, 4 * LANES), lambda b, i: (i, 0)),
                  const((1, D)), const((D, IN_PAD)), const((1, Q_LORA)), const((Q_LORA, 2048)),
                  const((1, KV_LORA)), const((KV_LORA, 2048))],
        out_specs=[tspec(w) for w, _ in widths],
        out_shape=[tok(w, dt) for w, dt in widths],
        compiler_params=_params("parallel", "arbitrary"),
        name="inproj",
    )(x, mod, tab, g_attn.reshape(1, D), win, qn_g.reshape(1, Q_LORA), wq, kvn_g.reshape(1, KV_LORA), wkv)


N_CHAIN = MLA_HEADS + 2 * DIFF_HEADS
CHAIN_GROUPS = ((tuple(range(MLA_HEADS)), tuple(range(DIFF_HEADS))),)


def _softmax_update(s, v, m_ref, l_ref, acc_ref):
    reps = s.shape[1] // LANES
    m_prev = m_ref[...]
    m_next = jnp.maximum(m_prev, jnp.max(s, axis=1, keepdims=True))
    p = jnp.exp(s - jnp.tile(m_next, (1, reps)))
    alpha = jnp.exp(m_prev - m_next)
    l_ref[...] = alpha * l_ref[...] + jnp.sum(p, axis=1, keepdims=True)
    m_ref[...] = m_next
    acc_ref[...] = alpha * acc_ref[...] + _dot(p.astype(BF16), v)


def _attn_kernel(lam_ref, q_ref, k_ref, vlo_ref, vhi_ref, dq_ref, dk_ref, dv_ref, subg_ref,
                 om_ref, od_ref, qd_ref, m_ref, l_ref, acc_ref, *, tq, tk, lam_init):
    q0 = pl.program_id(1) * tq
    n_full = q0 // tk
    lam = lam_ref[0]

    m_ref[...] = jnp.full(m_ref.shape, NEG_INF, F32)
    l_ref[...] = jnp.zeros(l_ref.shape, F32)
    acc_ref[...] = jnp.zeros(acc_ref.shape, F32)
    lane = lax.broadcasted_iota(jnp.int32, (tq, LANES), 1)
    for hd in range(DIFF_HEADS):
        qd = dq_ref[0, :, hd * LANES:(hd + 1) * LANES]
        zero = jnp.zeros_like(qd)
        qd_ref[2 * hd] = jnp.where(lane < DIFF_QK, qd, zero)
        qd_ref[2 * hd + 1] = jnp.where(lane >= DIFF_QK, qd, zero)

    col = lax.broadcasted_iota(jnp.int32, (tq, tk), 1)
    row = lax.broadcasted_iota(jnp.int32, (tq, tk), 0)

    def chunk(j, diagonal, mla_heads, diff_heads):
        off = pl.multiple_of(j * tk, tk)
        rows = lambda ref, lane0: ref[0, pl.ds(off, tk), lane0:lane0 + LANES]
        rel = col - row + (off - q0)
        for hd in mla_heads:
            s = _dot_nt(q_ref[0, :, hd * LANES:(hd + 1) * LANES], rows(k_ref, hd * LANES))
            if diagonal:
                s = jnp.where(rel <= 0, s, NEG_INF)
            v = rows(vlo_ref if hd % 2 == 0 else vhi_ref, (hd // 2) * LANES)
            _softmax_update(s, v, m_ref.at[hd], l_ref.at[hd], acc_ref.at[hd])
        dist = rel.astype(F32)
        if diagonal:
            dist = jnp.where(rel <= 0, dist, NEG_INF)
        for hd in diff_heads:
            kv = hd // DIFF_GROUP
            bias = dist * (2.0 ** (-8.0 * (hd + 1) / DIFF_HEADS))
            kk = rows(dk_ref, kv * LANES)
            vv = rows(dv_ref, kv * LANES)
            for mp in range(2):
                c = MLA_HEADS + 2 * hd + mp
                s = _dot_nt(qd_ref[2 * hd + mp], kk) + bias
                _softmax_update(s, vv, m_ref.at[c], l_ref.at[c], acc_ref.at[c])

    for mla_heads, diff_heads in CHAIN_GROUPS:
        def body(j, carry, mla_heads=mla_heads, diff_heads=diff_heads):
            chunk(j, False, mla_heads, diff_heads)
            return carry

        lax.fori_loop(0, n_full, body, 0)
        chunk(n_full, True, mla_heads, diff_heads)

    inv = lambda c: acc_ref[c] / l_ref[c]
    for pair in range(MLA_HEADS // 2):
        om_ref[0, :, pair * LANES:(pair + 1) * LANES] = (inv(2 * pair) + inv(2 * pair + 1)).astype(BF16)
    for hd in range(DIFF_HEADS):
        c = MLA_HEADS + 2 * hd
        o = inv(c) - lam * inv(c + 1)
        od_ref[0, :, hd * LANES:(hd + 1) * LANES] = (_rms(o, subg_ref[...]) * (1.0 - lam_init)).astype(BF16)


def _prompt_attention(lam, q, k, vlo, vhi, dq, dkb, dvb, subln_g, lam_init, tq):
    nb, s, _ = q.shape
    qspec = lambda w: pl.BlockSpec((1, tq, w), lambda b, i: (b, i, 0))
    full = lambda w: pl.BlockSpec((1, s, w), lambda b, i: (b, 0, 0))
    state = pltpu.VMEM((N_CHAIN, tq, LANES), F32)
    return pl.pallas_call(
        functools.partial(_attn_kernel, tq=tq, tk=tq, lam_init=lam_init),
        grid=(nb, s // tq),
        in_specs=[pl.BlockSpec(memory_space=pltpu.SMEM), qspec(1024), full(1024), full(512), full(512),
                  qspec(1024), full(256), full(256), pl.BlockSpec((1, DIFF_V), lambda b, i: (0, 0))],
        out_specs=[qspec(512), qspec(1024)],
        out_shape=[jax.ShapeDtypeStruct((nb, s, 512), BF16), jax.ShapeDtypeStruct((nb, s, 1024), BF16)],
        scratch_shapes=[pltpu.VMEM((2 * DIFF_HEADS, tq, LANES), BF16), state, state, state],
        compiler_params=_params("parallel", "arbitrary"),
        name="prompt_attn",
    )(lam, q, k, vlo, vhi, dq, dkb, dvb, subln_g.reshape(1, DIFF_V))


def _decode_kernel(pt_ref, lam_ref, qlat_ref, qd_ref, nckv_ref, nkr_ref, ndk_ref, ndv_ref, subg_ref, *refs,
                   pps, past_len, lam_init):
    pages = refs[:4 * pps]
    olat_ref, od_ref = refs[4 * pps], refs[4 * pps + 1]
    m_m, l_m, a_m, m_d, l_d, a_d = refs[4 * pps + 2:]
    step = pl.program_id(1)
    n_steps = pl.num_programs(1)
    R = MLA_HEADS
    RD = DIFF_KV_HEADS * R
    W2 = DIFF_KV_HEADS * PAGE_SIZE

    qlat = qlat_ref[0]
    qd = qd_ref[0]
    kv_row = lax.broadcasted_iota(jnp.int32, (RD, LANES), 0) // R

    @pl.when(step == 0)
    def _():
        s = (jnp.sum(qlat[:, :KV_LORA] * nckv_ref[0], axis=-1, keepdims=True)
             + jnp.sum(qlat[:, KV_LORA:KV_LORA + MLA_ROPE] * nkr_ref[0], axis=-1, keepdims=True))
        m_m[...] = jnp.broadcast_to(s, (R, LANES))
        l_m[...] = jnp.ones((R, LANES), F32)
        a_m[...] = jnp.broadcast_to(nckv_ref[0], (R, KV_LORA))
        ndk, ndv = ndk_ref[0], ndv_ref[0]
        own = lambda t: jnp.where(kv_row == 0, jnp.broadcast_to(t[:, :LANES], (RD, LANES)),
                                  jnp.broadcast_to(t[:, LANES:], (RD, LANES)))
        m_d[...] = jnp.broadcast_to(jnp.sum(qd * own(ndk), axis=-1, keepdims=True), (RD, LANES))
        l_d[...] = jnp.ones((RD, LANES), F32)
        a_d[...] = own(ndv)

    def update(s_list, v_list, m_ref, l_ref, a_ref):
        m_old = m_ref[...]
        mx = s_list[0]
        for s in s_list[1:]:
            mx = jnp.maximum(mx, s)
        m_new = jnp.maximum(m_old, jnp.max(mx, axis=-1, keepdims=True))
        alpha = jnp.exp(m_old - m_new)
        reps = s_list[0].shape[1] // LANES
        m_b = jnp.tile(m_new, (1, reps))
        psum = None
        pv = None
        for s, v in zip(s_list, v_list):
            p = jnp.exp(s - m_b)
            psum = p if psum is None else psum + p
            d = _dot(p.astype(BF16), v)
            pv = d if pv is None else pv + d
        m_ref[...] = m_new
        l_ref[...] = alpha * l_ref[...] + jnp.sum(psum, axis=-1, keepdims=True)
        a_ref[...] = jnp.tile(alpha, (1, a_ref.shape[1] // LANES)) * a_ref[...] + pv

    qc = qlat[:, :KV_LORA].astype(BF16)
    qr = qlat[:, KV_LORA:KV_LORA + MLA_ROPE].astype(BF16)
    ckvs = [pages[4 * j][0, 0].astype(BF16) for j in range(pps)]
    s_m = [_dot_nt(qc, ckvs[j]) + _dot(qr, pages[4 * j + 1][0, 0].astype(BF16)) for j in range(pps)]
    update(s_m, ckvs, m_m, l_m, a_m)

    col = lax.broadcasted_iota(jnp.int32, (RD, W2), 1)
    row = lax.broadcasted_iota(jnp.int32, (RD, W2), 0)
    hd = (row // R) * DIFF_GROUP + row % DIFF_GROUP
    slope = jnp.exp2(-8.0 * (hd + 1).astype(F32) / DIFF_HEADS)
    base = jnp.where(col % DIFF_KV_HEADS == row // R, slope * (col // DIFF_KV_HEADS).astype(F32), NEG_INF)
    qdb = qd.astype(BF16)
    s_d, v_d = [], []
    for j in range(pps):
        dist0 = ((step * pps + j) * PAGE_SIZE - past_len).astype(F32)
        s_d.append(_dot_nt(qdb, pages[4 * j + 2][0, 0].astype(BF16)) + (base + slope * dist0))
        v_d.append(pages[4 * j + 3][0, 0].astype(BF16))
    update(s_d, v_d, m_d, l_d, a_d)

    @pl.when(step == n_steps - 1)
    def _():
        olat_ref[0] = a_m[...] / jnp.tile(l_m[...], (1, KV_LORA // LANES))
        lam = lam_ref[0]
        o2 = a_d[...] / l_d[...]
        for kv in range(DIFF_KV_HEADS):
            o = o2[kv * R:kv * R + DIFF_GROUP] - lam * o2[kv * R + DIFF_GROUP:(kv + 1) * R]
            od_ref[0, kv] = _rms(o, subg_ref[...]) * (1.0 - lam_init)


def _decode_attention(page_table, lam, qlat, qd, nckv, nkr, ndk, ndv, subln_g, cache_ckv, cache_kr,
                      cache_dk, cache_dv, layer, lam_init, pps):
    n_seq, n_pages = page_table.shape
    past_len = n_pages * PAGE_SIZE
    R = MLA_HEADS
    RD = DIFF_KV_HEADS * R
    W2 = DIFF_KV_HEADS * PAGE_SIZE
    n_layers, n_pool = cache_ckv.shape[:2]
    cache_krt = jnp.swapaxes(cache_kr, 2, 3)
    cache_dk2 = cache_dk.reshape(n_layers, n_pool, W2, 2 * DIFF_QK)
    cache_dv2 = cache_dv.reshape(n_layers, n_pool, W2, DIFF_V)
    seq3 = lambda w: pl.BlockSpec((1, 1, w), lambda n, s, pt: (n, 0, 0))
    in_specs = [pl.BlockSpec(memory_space=pltpu.SMEM),
                pl.BlockSpec((1, R, 384), lambda n, s, pt: (n, 0, 0)),
                pl.BlockSpec((1, RD, LANES), lambda n, s, pt: (n, 0, 0)),
                seq3(KV_LORA), seq3(MLA_ROPE), seq3(256), seq3(256),
                pl.BlockSpec((1, DIFF_V), lambda n, s, pt: (0, 0))]
    args = [lam, qlat, qd, nckv, nkr, ndk, ndv, subln_g.reshape(1, DIFF_V)]
    for j in range(pps):
        page_map = lambda n, s, pt, j=j: (layer, pt[n, s * pps + j], 0, 0)
        in_specs += [pl.BlockSpec((1, 1, PAGE_SIZE, KV_LORA), page_map),
                     pl.BlockSpec((1, 1, MLA_ROPE, PAGE_SIZE), page_map),
                     pl.BlockSpec((1, 1, W2, 2 * DIFF_QK), page_map),
                     pl.BlockSpec((1, 1, W2, DIFF_V), page_map)]
        args += [cache_ckv, cache_krt, cache_dk2, cache_dv2]
    grid_spec = pltpu.PrefetchScalarGridSpec(
        num_scalar_prefetch=1,
        grid=(n_seq, n_pages // pps),
        in_specs=in_specs,
        out_specs=[pl.BlockSpec((1, R, KV_LORA), lambda n, s, pt: (n, 0, 0)),
                   pl.BlockSpec((1, DIFF_KV_HEADS, DIFF_GROUP, LANES), lambda n, s, pt: (n, 0, 0, 0))],
        scratch_shapes=[pltpu.VMEM((R, LANES), F32), pltpu.VMEM((R, LANES), F32), pltpu.VMEM((R, KV_LORA), F32),
                        pltpu.VMEM((RD, LANES), F32), pltpu.VMEM((RD, LANES), F32), pltpu.VMEM((RD, LANES), F32)],
    )
    return pl.pallas_call(
        functools.partial(_decode_kernel, pps=pps, past_len=past_len, lam_init=lam_init),
        grid_spec=grid_spec,
        out_shape=[jax.ShapeDtypeStruct((n_seq, R, KV_LORA), F32),
                   jax.ShapeDtypeStruct((n_seq, DIFF_KV_HEADS, DIFF_GROUP, LANES), F32)],
        compiler_params=_params("parallel", "arbitrary"),
        name="decode_attn",
    )(page_table, *args)


def _post_kernel(om_ref, od_ref, gates_ref, x_ref, mod_ref, wom_ref, wod_ref, wout_ref, gffn_ref, wpq_ref,
                 keys_ref, x1_ref, h2t_ref, st_ref):
    D = D_MODEL
    mod = mod_ref[0]
    gates = gates_ref[...]
    merged = gates[:, :D] * _dot(om_ref[...], wom_ref[...]) + gates[:, D:] * _dot(od_ref[...], wod_ref[...])
    x1 = x_ref[...] + mod[:, 2 * D:3 * D] * _dot(merged.astype(BF16), wout_ref[...])
    x1_ref[...] = x1
    h2 = _rms(x1, gffn_ref[...]) * (1.0 + mod[:, 4 * D:5 * D]) + mod[:, 3 * D:4 * D]
    h2t_ref[...] = h2.T.astype(BF16)
    pq = _dot(h2.astype(BF16), wpq_ref[...]).astype(BF16)
    for i in range(2 * PEER_HEADS):
        st_ref[i] = _dot_nt(keys_ref[i], pq[:, i * LANES:(i + 1) * LANES])


def _post(om, od, gates, x, mod, wom, wod, wout, g_ffn, wpq, keys, tm, tiles_per_row):
    t, D = x.shape
    mrows = mod.shape[1]
    tok = lambda w: pl.BlockSpec((tm, w), lambda i: (i, 0))
    const = lambda shape: pl.BlockSpec(shape, lambda i: (0,) * len(shape))
    mod_spec = (pl.BlockSpec((1, 1, 6 * D), lambda i: (i // tiles_per_row, 0, 0)) if mrows == 1
                else pl.BlockSpec((1, tm, 6 * D), lambda i: (0, i, 0)))
    nk = 2 * PEER_HEADS
    return pl.pallas_call(
        _post_kernel,
        grid=(t // tm,),
        in_specs=[tok(512), tok(1024), tok(2048), tok(D), mod_spec, const((512, D)), const((1024, D)),
                  const((D, D)), const((1, D)), const((D, PEER_HEADS * PEER_DKEY)),
                  const((nk, PEER_NKEYS, PEER_DKEY // 2))],
        out_specs=[tok(D), pl.BlockSpec((D, tm), lambda i: (0, i)),
                   pl.BlockSpec((nk, PEER_NKEYS, tm), lambda i: (0, 0, i))],
        out_shape=[jax.ShapeDtypeStruct((t, D), F32), jax.ShapeDtypeStruct((D, t), BF16),
                   jax.ShapeDtypeStruct((nk, PEER_NKEYS, t), F32)],
        compiler_params=_params("parallel"),
        name="post_attn",
    )(om, od, gates, x, mod, wom, wod, wout, g_ffn.reshape(1, D), wpq, keys)


N_EXT = PEER_TOPK + 1
CAND = [(i, j) for i in range(N_EXT) for j in range(N_EXT) if (i + 1) * (j + 1) <= N_EXT]


def _extract_desc(s, n):
    out = []
    for _ in range(n):
        m = jnp.max(s, axis=0, keepdims=True)
        out.append(m)
        s = jnp.where(s == m, NEG_INF, s)
    return out


CAND_ROWS = -(-len(CAND) // 8) * 8


def _select_kernel(st_ref, stats_ref, cand_ref):
    te = st_ref.shape[-1]
    cand_ref[...] = jnp.full(cand_ref.shape, NEG_INF, F32)
    stats_ref[...] = jnp.zeros(stats_ref.shape, F32)

    def head(h, carry):
        t1 = _extract_desc(st_ref[2 * h], N_EXT)
        t2 = _extract_desc(st_ref[2 * h + 1], N_EXT)
        for r, (i, j) in enumerate(CAND):
            cand_ref[r:r + 1, :] = t1[i] + t2[j]
        top = _extract_desc(cand_ref[...], N_EXT)
        z = jnp.ones((1, te), F32)
        for c in top[1:PEER_TOPK]:
            z = z + jnp.exp(c - top[0])
        stats_ref[h, 0:1, :] = 0.5 * (top[PEER_TOPK - 1] + top[PEER_TOPK])
        stats_ref[h, 1:2, :] = t1[0]
        stats_ref[h, 2:3, :] = t2[0]
        stats_ref[h, 3:4, :] = 1.0 / z
        return carry

    lax.fori_loop(0, PEER_HEADS, head, 0)


def _peer_select(st, te):
    nk, n, t = st.shape
    return pl.pallas_call(
        _select_kernel,
        grid=(t // te,),
        in_specs=[pl.BlockSpec((nk, n, te), lambda i: (0, 0, i))],
        out_specs=pl.BlockSpec((PEER_HEADS, 8, te), lambda i: (0, 0, i)),
        out_shape=jax.ShapeDtypeStruct((PEER_HEADS, 8, t), F32),
        scratch_shapes=[pltpu.VMEM((CAND_ROWS, te), F32)],
        compiler_params=_params("parallel"),
        name="peer_select",
    )(st)


SUB = 8
PAIR = 2 * PEER_NKEYS


def _peer_kernel(h2t_ref, st_ref, stats_ref, pu_ref, pvt_ref, x1_ref, mod_ref, gfin_ref, y_ref,
                 acc_ref, e2_ref, tau_ref, cc_ref, ut_ref, wt_ref, *, ca):
    D = D_MODEL
    c = pl.program_id(1)
    tm = h2t_ref.shape[1]
    n_pair = ca // 2

    @pl.when(c == 0)
    def _():
        acc_ref[...] = jnp.zeros_like(acc_ref)
        for h in range(PEER_HEADS):
            st = stats_ref[h]
            s1 = st_ref[2 * h]
            e2_ref[h] = jnp.exp(st_ref[2 * h + 1] - st[2:3])
            tau_ref[h] = st[0:1] - s1
            cc_ref[h] = jnp.exp(s1 - st[1:2]) * (0.5 * st[3:4])

    def first_matmul(p):
        ut_ref[p % 2] = _dot(pu_ref[p * PAIR:(p + 1) * PAIR, :], h2t_ref[...])

    def gate_rows(p, ah):
        al = 2 * p + ah
        a8 = pl.multiple_of(c * ca + al // SUB * SUB, SUB)
        row_of = lambda ref, h, lanes: jnp.broadcast_to(
            ref[h, pl.ds(a8, SUB), lanes][al % SUB:al % SUB + 1], (SUB, LANES))
        for tg in range(tm // LANES):
            lanes = slice(tg * LANES, (tg + 1) * LANES)
            taus = [row_of(tau_ref, h, lanes) for h in range(PEER_HEADS)]
            ccs = [row_of(cc_ref, h, lanes) for h in range(PEER_HEADS)]
            for bp in range(PEER_NKEYS // (2 * SUB)):
                ws = []
                for r0 in (2 * SUB * bp, 2 * SUB * bp + SUB):
                    g = None
                    for h in range(PEER_HEADS):
                        t = jnp.where(st_ref[2 * h + 1, r0:r0 + SUB, lanes] >= taus[h],
                                      e2_ref[h, r0:r0 + SUB, lanes] * ccs[h], 0.0)
                        g = t if g is None else g + t
                    u = ut_ref[p % 2, ah * PEER_NKEYS + r0:ah * PEER_NKEYS + r0 + SUB, lanes]
                    ws.append((g * u) * (1.0 + lax.erf(u * math.sqrt(0.5))))
                w0 = p * PAIR + ah * PEER_NKEYS + 2 * SUB * bp
                wt_ref[w0:w0 + 2 * SUB, lanes] = jnp.concatenate(ws, axis=0).astype(BF16)

    first_matmul(0)
    for p in range(n_pair):
        if p + 1 < n_pair:
            first_matmul(p + 1)
        gate_rows(p, 0)
        gate_rows(p, 1)
        if p % 2 == 1:
            lo, hi = (p - 1) * PAIR, (p + 1) * PAIR
            acc_ref[...] += _dot(pvt_ref[:, lo:hi], wt_ref[lo:hi, :])

    @pl.when(c == pl.num_programs(1) - 1)
    def _():
        mod = mod_ref[0]
        x2 = x1_ref[...] + mod[:, 5 * D:6 * D] * acc_ref[...].T
        y_ref[...] = _rms(x2, gfin_ref[...])


def _peer_dense(h2t, st, stats, pu, pvt, x1, mod, g_final, tm, ca, tiles_per_row):
    t, D = x1.shape
    mrows = mod.shape[1]
    nch = ca * PEER_NKEYS
    nk = 2 * PEER_HEADS
    mod_spec = (pl.BlockSpec((1, 1, 6 * D), lambda i, c: (i // tiles_per_row, 0, 0)) if mrows == 1
                else pl.BlockSpec((1, tm, 6 * D), lambda i, c: (0, i, 0)))
    return pl.pallas_call(
        functools.partial(_peer_kernel, ca=ca),
        grid=(t // tm, PEER_N // nch),
        in_specs=[pl.BlockSpec((D, tm), lambda i, c: (0, i)),
                  pl.BlockSpec((nk, PEER_NKEYS, tm), lambda i, c: (0, 0, i)),
                  pl.BlockSpec((PEER_HEADS, 8, tm), lambda i, c: (0, 0, i)),
                  pl.BlockSpec((nch, D), lambda i, c: (c, 0)),
                  pl.BlockSpec((D, nch), lambda i, c: (0, c)),
                  pl.BlockSpec((tm, D), lambda i, c: (i, 0)),
                  mod_spec,
                  pl.BlockSpec((1, D), lambda i, c: (0, 0))],
        out_specs=pl.BlockSpec((tm, D), lambda i, c: (i, 0)),
        out_shape=jax.ShapeDtypeStruct((t, D), F32),
        scratch_shapes=[pltpu.VMEM((D, tm), F32), pltpu.VMEM((PEER_HEADS, PEER_NKEYS, tm), F32),
                        pltpu.VMEM((PEER_HEADS, PEER_NKEYS, tm), F32),
                        pltpu.VMEM((PEER_HEADS, PEER_NKEYS, tm), F32), pltpu.VMEM((2, PAIR, tm), F32),
                        pltpu.VMEM((nch, tm), BF16)],
        compiler_params=_params("parallel", "arbitrary"),
        name="peer_dense",
    )(h2t, st, stats, pu, pvt, x1, mod, g_final.reshape(1, D))


def _rot_half(w):
    half = MLA_ROPE // 2
    return jnp.concatenate([-w[..., half:], w[..., :half]], axis=-1)


def _prep_layer(w_in, w_uq, w_uk, w_uv):
    D = D_MODEL
    o_kr = Q_LORA + KV_LORA
    w_kr = w_in[:, o_kr:o_kr + MLA_ROPE]
    zpad = jnp.zeros((D, LANES - MLA_ROPE), F32)
    win = jnp.concatenate([w_in[:, :o_kr], w_in[:, o_kr + MLA_ROPE:], w_kr, zpad, _rot_half(w_kr), zpad],
                          axis=1).astype(BF16)
    H = MLA_HEADS
    r = w_uq.reshape(Q_LORA, H, MLA_NOPE + MLA_ROPE)
    nope, rope = r[..., :MLA_NOPE], r[..., MLA_NOPE:]
    z32 = jnp.zeros((Q_LORA, H, LANES - MLA_NOPE - MLA_ROPE), F32)
    w_q = jnp.concatenate([nope, rope, z32], axis=-1).reshape(Q_LORA, H * LANES)
    w_qr = jnp.concatenate([jnp.zeros_like(nope), _rot_half(rope), z32], axis=-1).reshape(Q_LORA, H * LANES)
    wq = jnp.concatenate([w_q, w_qr], axis=1).astype(BF16)
    uk = w_uk.reshape(KV_LORA, H, MLA_NOPE)
    w_k = jnp.concatenate([uk, jnp.zeros_like(uk)], axis=-1).reshape(KV_LORA, H * LANES)
    uv = w_uv.reshape(KV_LORA, H // 2, 2, MLA_V)
    zv = jnp.zeros((KV_LORA, H // 2, MLA_V), F32)
    v_lo = jnp.concatenate([uv[:, :, 0], zv], axis=-1).reshape(KV_LORA, H // 2 * LANES)
    v_hi = jnp.concatenate([zv, uv[:, :, 1]], axis=-1).reshape(KV_LORA, H // 2 * LANES)
    wkv = jnp.concatenate([w_k, v_lo, v_hi], axis=1).astype(BF16)
    eye_h = jnp.eye(H, dtype=F32)
    uk_t = jnp.transpose(uk, (1, 2, 0))
    top = jnp.concatenate([uk_t, jnp.zeros((H, MLA_NOPE, LANES), F32)], axis=-1)
    mid = jnp.concatenate([jnp.zeros((H, MLA_ROPE, KV_LORA), F32),
                           jnp.broadcast_to(jnp.eye(MLA_ROPE, LANES, dtype=F32), (H, MLA_ROPE, LANES))], axis=-1)
    bot = jnp.zeros((H, LANES - MLA_NOPE - MLA_ROPE, KV_LORA + LANES), F32)
    m_h = jnp.concatenate([top, mid, bot], axis=1)
    w_abs = jnp.einsum('hrc,hg->hrgc', m_h, eye_h).reshape(H * LANES, H * (KV_LORA + LANES))
    uv_h = jnp.transpose(w_uv.reshape(KV_LORA, H, MLA_V), (1, 0, 2))
    w_uvbd = jnp.einsum('hcv,hg->hcgv', uv_h, eye_h).reshape(H * KV_LORA, H * MLA_V)
    return win, wq, wkv, w_abs, w_uvbd


def _rope_table(pos):
    inv = ROPE_THETA ** (-jnp.arange(0, MLA_ROPE, 2, dtype=F32) / MLA_ROPE)
    ang = pos.astype(F32)[:, None] * inv[None, :]
    cos, sin = jnp.cos(ang), jnp.sin(ang)
    n = pos.shape[0]
    z = lambda w: jnp.zeros((n, w), F32)
    c_q = MLA_SCALE * jnp.concatenate([jnp.ones((n, MLA_NOPE), F32), cos, cos, z(32)], axis=1)
    s_q = MLA_SCALE * jnp.concatenate([z(MLA_NOPE), sin, sin, z(32)], axis=1)
    c_k = jnp.concatenate([cos, cos, z(96)], axis=1)
    s_k = jnp.concatenate([sin, sin, z(96)], axis=1)
    return jnp.concatenate([c_q, s_q, c_k, s_k], axis=1)


def _pick(n, prefs):
    for p in prefs:
        if n % p == 0:
            return p
    return n


def kernel(x_prompt, x_sample, c_prompt, c_sample, cache_ckv, cache_kr, cache_dk, cache_dv, page_table, w_ada, b_ada, norm_attn_g, norm_ffn_g, w_in, mla_qnorm_g, w_uq, mla_kvnorm_g, w_uk, w_uv, w_o_mla, diff_lq1, diff_lk1, diff_lq2, diff_lk2, diff_subln_g, w_o_diff, w_out, w_pq, peer_keys, peer_u, peer_v, norm_final_g):
    D = D_MODEL
    nb, s_len, _ = x_prompt.shape
    n_seq, dec_seq, _ = x_sample.shape
    assert dec_seq == 1
    depth = w_ada.shape[0]
    n_pages = page_table.shape[1]
    past_len = n_pages * PAGE_SIZE
    t_p = nb * s_len

    tab_p = _rope_table(jnp.arange(s_len))
    tab_s = _rope_table(jnp.full((n_seq,), past_len, jnp.int32))
    tm_in = _pick(s_len, (256, 128))
    tq = _pick(s_len, (256, 128))
    tm_post = _pick(s_len, (256, 128))
    te = _pick(s_len, (256, 128))
    tm_peer = _pick(s_len, (512, 256, 128))
    ca = 16
    pps = _pick(n_pages, (16, 8, 4, 2, 1))

    xp = x_prompt
    xs = x_sample.reshape(1, n_seq, D)
    c_all = jnp.concatenate([c_prompt, c_sample], axis=0)
    new_p, new_s = ([], [], [], []), ([], [], [], [])
    yp = ys = None
    for l in range(depth):
        lam_init = 0.8 - 0.6 * math.exp(-0.3 * l)
        lam = (jnp.exp(jnp.sum(diff_lq1[l] * diff_lk1[l])) - jnp.exp(jnp.sum(diff_lq2[l] * diff_lk2[l]))
               + lam_init).astype(F32).reshape(1)
        win, wq, wkv, w_abs, w_uvbd = _prep_layer(w_in[l], w_uq[l], w_uk[l], w_uv[l])
        wom, wod, wout = w_o_mla[l].astype(BF16), w_o_diff[l].astype(BF16), w_out[l].astype(BF16)
        wpq = w_pq[l].astype(BF16)
        keys = peer_keys[l].reshape(2 * PEER_HEADS, PEER_NKEYS, PEER_DKEY // 2).astype(BF16)
        pu = peer_u[l].astype(BF16)
        pvt = peer_v[l].T.astype(BF16)
        last = l == depth - 1
        g_fin = norm_final_g if last else jnp.ones((D,), F32)

        mod = _small_mm(c_all, w_ada[l], b_ada[l], pre_silu=True, tn=1024)
        mod_p = mod[:nb].reshape(nb, 1, 6 * D)
        mod_s = mod[nb:].reshape(1, n_seq, 6 * D)

        def mixer(x, modx, tab, tm):
            return _inproj(x, modx, tab, norm_attn_g[l], win, mla_qnorm_g[l], wq, mla_kvnorm_g[l], wkv, tm)

        def ffn(om, od, gates, x2d, modx, tm_a, tm_b, te_, tiles_per_row):
            x1, h2t, st = _post(om, od, gates, x2d, modx, wom, wod, wout, norm_ffn_g[l], wpq, keys, tm_a,
                                max(tiles_per_row * tm_b // tm_a, 1))
            stats = _peer_select(st, te_)
            return _peer_dense(h2t, st, stats, pu, pvt, x1, modx, g_fin, tm_b, ca, tiles_per_row)

        (ckv, kr, dk, dv, q, k, vlo, vhi, dq, dkb, dvb, gates) = mixer(xp, mod_p, tab_p, tm_in)
        om, od = _prompt_attention(lam, q, k, vlo, vhi, dq, dkb, dvb, diff_subln_g[l], lam_init, tq)
        yp = ffn(om.reshape(t_p, 512), od.reshape(t_p, 1024), gates.reshape(t_p, 2048), xp.reshape(t_p, D),
                 mod_p, tm_post, tm_peer, te, s_len // tm_peer)
        for lst, r in zip(new_p, (ckv, kr, dk.reshape(nb, s_len, DIFF_KV_HEADS, 2 * DIFF_QK),
                                  dv.reshape(nb, s_len, DIFF_KV_HEADS, DIFF_V))):
            lst.append(r)

        (ckv_s, kr_s, dk_s, dv_s, q_s, _, _, _, dq_s, _, _, gates_s) = mixer(xs, mod_s, tab_s, n_seq)
        qlat = _small_mm(q_s[0].astype(F32), w_abs, tn=1024).reshape(n_seq, MLA_HEADS, KV_LORA + LANES)
        dq4 = dq_s[0].astype(F32).reshape(n_seq, DIFF_KV_HEADS, DIFF_GROUP, 2, DIFF_QK)
        zq = jnp.zeros_like(dq4[:, :, :, 0])
        qd = jnp.concatenate([jnp.concatenate([dq4[:, :, :, 0], zq], axis=-1),
                              jnp.concatenate([zq, dq4[:, :, :, 1]], axis=-1)],
                             axis=2).reshape(n_seq, DIFF_KV_HEADS * MLA_HEADS, LANES)
        olat, od_s = _decode_attention(
            page_table, lam, qlat, qd, ckv_s.reshape(n_seq, 1, KV_LORA), kr_s.reshape(n_seq, 1, MLA_ROPE),
            dk_s.reshape(n_seq, 1, 256), dv_s.reshape(n_seq, 1, 256), diff_subln_g[l],
            cache_ckv, cache_kr, cache_dk, cache_dv, l, lam_init, pps)
        om_s = _small_mm(olat.reshape(n_seq, MLA_HEADS * KV_LORA), w_uvbd).astype(BF16)
        ys = ffn(om_s, od_s.reshape(n_seq, 1024).astype(BF16), gates_s[0], xs[0], mod_s, n_seq, n_seq, n_seq, 1)
        for lst, r in zip(new_s, (ckv_s.reshape(n_seq, 1, KV_LORA), kr_s.reshape(n_seq, 1, MLA_ROPE),
                                  dk_s.reshape(n_seq, 1, DIFF_KV_HEADS, 2 * DIFF_QK),
                                  dv_s.reshape(n_seq, 1, DIFF_KV_HEADS, DIFF_V))):
            lst.append(r)
        if not last:
            raise NotImplementedError("multi-layer stacking needs the un-normalised residual stream")

    y_prompt = yp.reshape(nb, s_len, D)
    y_sample = ys.reshape(n_seq, 1, D)
    stack = lambda t: jnp.stack(t, axis=0)
    return (y_prompt, y_sample, *[stack(t) for t in new_p], *[stack(t) for t in new_s])
```

```python
import functools
import math

import jax
import jax.numpy as jnp
from jax import lax
from jax.experimental import pallas as pl
from jax.experimental.pallas import tpu as pltpu

D_MODEL = 1024
PAGE_SIZE = 128
MLA_HEADS = 8
MLA_NOPE = 64
MLA_ROPE = 32
MLA_V = 64
Q_LORA = 384
KV_LORA = 256
ROPE_THETA = 10000.0
MLA_SCALE = (MLA_NOPE + MLA_ROPE) ** -0.5
DIFF_HEADS = 8
DIFF_KV_HEADS = 2
DIFF_GROUP = DIFF_HEADS // DIFF_KV_HEADS
DIFF_QK = 64
DIFF_V = 2 * DIFF_QK
PEER_HEADS = 8
PEER_NKEYS = 128
PEER_N = PEER_NKEYS * PEER_NKEYS
PEER_DKEY = 256
PEER_TOPK = 16
RMS_EPS = 1e-6

LANES = 128
VMEM_LIMIT = 56 * 1024 * 1024
IN_PAD = 4480
NEG_INF = float("-inf")

BF16 = jnp.bfloat16
F32 = jnp.float32


def _params(*sem):
    return pltpu.CompilerParams(dimension_semantics=sem, vmem_limit_bytes=VMEM_LIMIT)


def _dot(a, b):
    return jnp.dot(a, b, preferred_element_type=F32)


def _dot_nt(a, b):
    return lax.dot_general(a, b, (((1,), (1,)), ((), ())), preferred_element_type=F32)


def _rms(x, g):
    return x * lax.rsqrt(jnp.mean(x * x, axis=-1, keepdims=True) + RMS_EPS) * g


def _small_mm_kernel(a_ref, w_ref, b_ref, o_ref, *, pre_silu):
    a = a_ref[...]
    if pre_silu:
        a = a * jax.nn.sigmoid(a)
    o_ref[...] = _dot(a.astype(BF16), w_ref[...].astype(BF16)) + b_ref[...]


def _small_mm(a, w, bias=None, pre_silu=False, tn=512):
    m, k = a.shape
    n = w.shape[1]
    tn = min(tn, n)
    if bias is None:
        bias = jnp.zeros((n,), F32)
    return pl.pallas_call(
        functools.partial(_small_mm_kernel, pre_silu=pre_silu),
        grid=(n // tn,),
        in_specs=[pl.BlockSpec((m, k), lambda j: (0, 0)),
                  pl.BlockSpec((k, tn), lambda j: (0, j)),
                  pl.BlockSpec((1, tn), lambda j: (0, j))],
        out_specs=pl.BlockSpec((m, tn), lambda j: (0, j)),
        out_shape=jax.ShapeDtypeStruct((m, n), F32),
        compiler_params=_params("arbitrary"),
        name="small_mm",
    )(a, w, bias.reshape(1, n))


C_CQ, C_CKV, C_DQ, C_DK, C_DV, C_GA, C_GB, C_KR, C_KRR = 0, 384, 640, 1664, 1920, 2176, 3200, 4224, 4352


def _inproj_kernel(x_ref, mod_ref, tab_ref, gattn_ref, win_ref, qng_ref, wq_ref, kvg_ref, wkv_ref,
                   ckv_ref, kr_ref, dk_ref, dv_ref, q_ref, k_ref, vlo_ref, vhi_ref, dq_ref,
                   dkb_ref, dvb_ref, gates_ref):
    D = D_MODEL
    x = x_ref[0]
    mod = mod_ref[0]
    h = _rms(x, gattn_ref[...]) * (1.0 + mod[:, D:2 * D]) + mod[:, 0:D]
    hb = h.astype(BF16)
    tab = tab_ref[...]
    c_q, s_q, c_k, s_k = (tab[:, i * LANES:(i + 1) * LANES] for i in range(4))

    def proj(lo, hi):
        return _dot(hb, win_ref[:, lo:hi])

    gates_ref[0] = jax.nn.sigmoid(proj(C_GA, C_KR))
    dq_ref[0] = (proj(C_DQ, C_DK) * (DIFF_QK ** -0.5)).astype(BF16)
    dk = proj(C_DK, C_DV)
    dv = proj(C_DV, C_GA)
    dk_ref[0] = dk
    dv_ref[0] = dv
    dkb_ref[0] = dk.astype(BF16)
    dvb_ref[0] = dv.astype(BF16)

    kr2 = proj(C_KR, IN_PAD)
    kr = kr2[:, :LANES] * c_k + kr2[:, LANES:] * s_k
    kr_ref[0] = kr[:, :MLA_ROPE]
    kr_sh = pltpu.roll(kr, MLA_NOPE, 1)

    ckv = _rms(proj(C_CKV, C_DQ), kvg_ref[...])
    ckv_ref[0] = ckv
    kv = _dot(ckv.astype(BF16), wkv_ref[...])
    for hd in range(MLA_HEADS):
        sl = slice(hd * LANES, (hd + 1) * LANES)
        k_ref[0, :, sl] = (kv[:, sl] + kr_sh).astype(BF16)
    vlo_ref[0] = kv[:, 1024:1536].astype(BF16)
    vhi_ref[0] = kv[:, 1536:2048].astype(BF16)

    cq = _rms(proj(C_CQ, C_CKV), qng_ref[...]).astype(BF16)
    q2 = _dot(cq, wq_ref[...])
    for hd in range(MLA_HEADS):
        sl = slice(hd * LANES, (hd + 1) * LANES)
        sr = slice(1024 + hd * LANES, 1024 + (hd + 1) * LANES)
        q_ref[0, :, sl] = (q2[:, sl] * c_q + q2[:, sr] * s_q).astype(BF16)


def _inproj(x, mod, tab, g_attn, win, qn_g, wq, kvn_g, wkv, tm):
    nb, s, D = x.shape
    mrows = mod.shape[1]
    grid = (nb, s // tm)
    tok = lambda w, dt: jax.ShapeDtypeStruct((nb, s, w), dt)
    tspec = lambda w: pl.BlockSpec((1, tm, w), lambda b, i: (b, i, 0))
    const = lambda shape: pl.BlockSpec(shape, lambda b, i: (0,) * len(shape))
    mod_spec = (pl.BlockSpec((1, 1, 6 * D), lambda b, i: (b, 0, 0)) if mrows == 1
                else pl.BlockSpec((1, tm, 6 * D), lambda b, i: (b, i, 0)))
    widths = [(KV_LORA, F32), (MLA_ROPE, F32), (256, F32), (256, F32), (1024, BF16), (1024, BF16),
              (512, BF16), (512, BF16), (1024, BF16), (256, BF16), (256, BF16), (2048, F32)]
    return pl.pallas_call(
        _inproj_kernel,
        grid=grid,
        in_specs=[tspec(D), mod_spec, pl.BlockSpec((tm, 4 * LANES), lambda b, i: (i, 0)),
                  const((1, D)), const((D, IN_PAD)), const((1, Q_LORA)), const((Q_LORA, 2048)),
                  const((1, KV_LORA)), const((KV_LORA, 2048))],
        out_specs=[tspec(w) for w, _ in widths],
        out_shape=[tok(w, dt) for w, dt in widths],
        compiler_params=_params("parallel", "arbitrary"),
        name="inproj",
    )(x, mod, tab, g_attn.reshape(1, D), win, qn_g.reshape(1, Q_LORA), wq, kvn_g.reshape(1, KV_LORA), wkv)


N_CHAIN = MLA_HEADS + 2 * DIFF_HEADS
CHAIN_GROUPS = ((tuple(range(MLA_HEADS)), tuple(range(DIFF_HEADS))),)


def _softmax_update(s, v, m_ref, l_ref, acc_ref):
    reps = s.shape[1] // LANES
    m_prev = m_ref[...]
    m_next = jnp.maximum(m_prev, jnp.max(s, axis=1, keepdims=True))
    p = jnp.exp(s - jnp.tile(m_next, (1, reps)))
    alpha = jnp.exp(m_prev - m_next)
    l_ref[...] = alpha * l_ref[...] + jnp.sum(p, axis=1, keepdims=True)
    m_ref[...] = m_next
    acc_ref[...] = alpha * acc_ref[...] + _dot(p.astype(BF16), v)


def _attn_kernel(lam_ref, q_ref, k_ref, vlo_ref, vhi_ref, dq_ref, dk_ref, dv_ref, subg_ref,
                 om_ref, od_ref, qd_ref, m_ref, l_ref, acc_ref, *, tq, tk, lam_init):
    q0 = pl.program_id(1) * tq
    n_full = q0 // tk
    lam = lam_ref[0]

    m_ref[...] = jnp.full(m_ref.shape, NEG_INF, F32)
    l_ref[...] = jnp.zeros(l_ref.shape, F32)
    acc_ref[...] = jnp.zeros(acc_ref.shape, F32)
    lane = lax.broadcasted_iota(jnp.int32, (tq, LANES), 1)
    for hd in range(DIFF_HEADS):
        qd = dq_ref[0, :, hd * LANES:(hd + 1) * LANES]
        zero = jnp.zeros_like(qd)
        qd_ref[2 * hd] = jnp.where(lane < DIFF_QK, qd, zero)
        qd_ref[2 * hd + 1] = jnp.where(lane >= DIFF_QK, qd, zero)

    col = lax.broadcasted_iota(jnp.int32, (tq, tk), 1)
    row = lax.broadcasted_iota(jnp.int32, (tq, tk), 0)

    def chunk(j, diagonal, mla_heads, diff_heads):
        off = pl.multiple_of(j * tk, tk)
        rows = lambda ref, lane0: ref[0, pl.ds(off, tk), lane0:lane0 + LANES]
        rel = col - row + (off - q0)
        for hd in mla_heads:
            s = _dot_nt(q_ref[0, :, hd * LANES:(hd + 1) * LANES], rows(k_ref, hd * LANES))
            if diagonal:
                s = jnp.where(rel <= 0, s, NEG_INF)
            v = rows(vlo_ref if hd % 2 == 0 else vhi_ref, (hd // 2) * LANES)
            _softmax_update(s, v, m_ref.at[hd], l_ref.at[hd], acc_ref.at[hd])
        dist = rel.astype(F32)
        if diagonal:
            dist = jnp.where(rel <= 0, dist, NEG_INF)
        for hd in diff_heads:
            kv = hd // DIFF_GROUP
            bias = dist * (2.0 ** (-8.0 * (hd + 1) / DIFF_HEADS))
            kk = rows(dk_ref, kv * LANES)
            vv = rows(dv_ref, kv * LANES)
            for mp in range(2):
                c = MLA_HEADS + 2 * hd + mp
                s = _dot_nt(qd_ref[2 * hd + mp], kk) + bias
                _softmax_update(s, vv, m_ref.at[c], l_ref.at[c], acc_ref.at[c])

    for mla_heads, diff_heads in CHAIN_GROUPS:
        def body(j, carry, mla_heads=mla_heads, diff_heads=diff_heads):
            chunk(j, False, mla_heads, diff_heads)
            return carry

        lax.fori_loop(0, n_full, body, 0)
        chunk(n_full, True, mla_heads, diff_heads)

    inv = lambda c: acc_ref[c] / l_ref[c]
    for pair in range(MLA_HEADS // 2):
        om_ref[0, :, pair * LANES:(pair + 1) * LANES] = (inv(2 * pair) + inv(2 * pair + 1)).astype(BF16)
    for hd in range(DIFF_HEADS):
        c = MLA_HEADS + 2 * hd
        o = inv(c) - lam * inv(c + 1)
        od_ref[0, :, hd * LANES:(hd + 1) * LANES] = (_rms(o, subg_ref[...]) * (1.0 - lam_init)).astype(BF16)


def _prompt_attention(lam, q, k, vlo, vhi, dq, dkb, dvb, subln_g, lam_init, tq):
    nb, s, _ = q.shape
    qspec = lambda w: pl.BlockSpec((1, tq, w), lambda b, i: (b, i, 0))
    full = lambda w: pl.BlockSpec((1, s, w), lambda b, i: (b, 0, 0))
    state = pltpu.VMEM((N_CHAIN, tq, LANES), F32)
    return pl.pallas_call(
        functools.partial(_attn_kernel, tq=tq, tk=tq, lam_init=lam_init),
        grid=(nb, s // tq),
        in_specs=[pl.BlockSpec(memory_space=pltpu.SMEM), qspec(1024), full(1024), full(512), full(512),
                  qspec(1024), full(256), full(256), pl.BlockSpec((1, DIFF_V), lambda b, i: (0, 0))],
        out_specs=[qspec(512), qspec(1024)],
        out_shape=[jax.ShapeDtypeStruct((nb, s, 512), BF16), jax.ShapeDtypeStruct((nb, s, 1024), BF16)],
        scratch_shapes=[pltpu.VMEM((2 * DIFF_HEADS, tq, LANES), BF16), state, state, state],
        compiler_params=_params("parallel", "arbitrary"),
        name="prompt_attn",
    )(lam, q, k, vlo, vhi, dq, dkb, dvb, subln_g.reshape(1, DIFF_V))


def _decode_kernel(pt_ref, lam_ref, qlat_ref, qd_ref, nckv_ref, nkr_ref, ndk_ref, ndv_ref, subg_ref, *refs,
                   pps, past_len, lam_init):
    pages = refs[:4 * pps]
    olat_ref, od_ref = refs[4 * pps], refs[4 * pps + 1]
    m_m, l_m, a_m, m_d, l_d, a_d = refs[4 * pps + 2:]
    step = pl.program_id(1)
    n_steps = pl.num_programs(1)
    R = MLA_HEADS
    RD = DIFF_KV_HEADS * R
    W2 = DIFF_KV_HEADS * PAGE_SIZE

    qlat = qlat_ref[0]
    qd = qd_ref[0]
    kv_row = lax.broadcasted_iota(jnp.int32, (RD, LANES), 0) // R

    @pl.when(step == 0)
    def _():
        s = (jnp.sum(qlat[:, :KV_LORA] * nckv_ref[0], axis=-1, keepdims=True)
             + jnp.sum(qlat[:, KV_LORA:KV_LORA + MLA_ROPE] * nkr_ref[0], axis=-1, keepdims=True))
        m_m[...] = jnp.broadcast_to(s, (R, LANES))
        l_m[...] = jnp.ones((R, LANES), F32)
        a_m[...] = jnp.broadcast_to(nckv_ref[0], (R, KV_LORA))
        ndk, ndv = ndk_ref[0], ndv_ref[0]
        own = lambda t: jnp.where(kv_row == 0, jnp.broadcast_to(t[:, :LANES], (RD, LANES)),
                                  jnp.broadcast_to(t[:, LANES:], (RD, LANES)))
        m_d[...] = jnp.broadcast_to(jnp.sum(qd * own(ndk), axis=-1, keepdims=True), (RD, LANES))
        l_d[...] = jnp.ones((RD, LANES), F32)
        a_d[...] = own(ndv)

    def update(s_list, v_list, m_ref, l_ref, a_ref):
        m_old = m_ref[...]
        mx = s_list[0]
        for s in s_list[1:]:
            mx = jnp.maximum(mx, s)
        m_new = jnp.maximum(m_old, jnp.max(mx, axis=-1, keepdims=True))
        alpha = jnp.exp(m_old - m_new)
        reps = s_list[0].shape[1] // LANES
        m_b = jnp.tile(m_new, (1, reps))
        psum = None
        pv = None
        for s, v in zip(s_list, v_list):
            p = jnp.exp(s - m_b)
            psum = p if psum is None else psum + p
            d = _dot(p.astype(BF16), v)
            pv = d if pv is None else pv + d
        m_ref[...] = m_new
        l_ref[...] = alpha * l_ref[...] + jnp.sum(psum, axis=-1, keepdims=True)
        a_ref[...] = jnp.tile(alpha, (1, a_ref.shape[1] // LANES)) * a_ref[...] + pv

    qc = qlat[:, :KV_LORA].astype(BF16)
    qr = qlat[:, KV_LORA:KV_LORA + MLA_ROPE].astype(BF16)
    ckvs = [pages[4 * j][0, 0].astype(BF16) for j in range(pps)]
    s_m = [_dot_nt(qc, ckvs[j]) + _dot(qr, pages[4 * j + 1][0, 0].astype(BF16)) for j in range(pps)]
    update(s_m, ckvs, m_m, l_m, a_m)

    col = lax.broadcasted_iota(jnp.int32, (RD, W2), 1)
    row = lax.broadcasted_iota(jnp.int32, (RD, W2), 0)
    hd = (row // R) * DIFF_GROUP + row % DIFF_GROUP
    slope = jnp.exp2(-8.0 * (hd + 1).astype(F32) / DIFF_HEADS)
    base = jnp.where(col % DIFF_KV_HEADS == row // R, slope * (col // DIFF_KV_HEADS).astype(F32), NEG_INF)
    qdb = qd.astype(BF16)
    s_d, v_d = [], []
    for j in range(pps):
        dist0 = ((step * pps + j) * PAGE_SIZE - past_len).astype(F32)
        s_d.append(_dot_nt(qdb, pages[4 * j + 2][0, 0].astype(BF16)) + (base + slope * dist0))
        v_d.append(pages[4 * j + 3][0, 0].astype(BF16))
    update(s_d, v_d, m_d, l_d, a_d)

    @pl.when(step == n_steps - 1)
    def _():
        olat_ref[0] = a_m[...] / jnp.tile(l_m[...], (1, KV_LORA // LANES))
        lam = lam_ref[0]
        o2 = a_d[...] / l_d[...]
        for kv in range(DIFF_KV_HEADS):
            o = o2[kv * R:kv * R + DIFF_GROUP] - lam * o2[kv * R + DIFF_GROUP:(kv + 1) * R]
            od_ref[0, kv] = _rms(o, subg_ref[...]) * (1.0 - lam_init)


def _decode_attention(page_table, lam, qlat, qd, nckv, nkr, ndk, ndv, subln_g, cache_ckv, cache_kr,
                      cache_dk, cache_dv, layer, lam_init, pps):
    n_seq, n_pages = page_table.shape
    past_len = n_pages * PAGE_SIZE
    R = MLA_HEADS
    RD = DIFF_KV_HEADS * R
    W2 = DIFF_KV_HEADS * PAGE_SIZE
    n_layers, n_pool = cache_ckv.shape[:2]
    cache_krt = jnp.swapaxes(cache_kr, 2, 3)
    cache_dk2 = cache_dk.reshape(n_layers, n_pool, W2, 2 * DIFF_QK)
    cache_dv2 = cache_dv.reshape(n_layers, n_pool, W2, DIFF_V)
    seq3 = lambda w: pl.BlockSpec((1, 1, w), lambda n, s, pt: (n, 0, 0))
    in_specs = [pl.BlockSpec(memory_space=pltpu.SMEM),
                pl.BlockSpec((1, R, 384), lambda n, s, pt: (n, 0, 0)),
                pl.BlockSpec((1, RD, LANES), lambda n, s, pt: (n, 0, 0)),
                seq3(KV_LORA), seq3(MLA_ROPE), seq3(256), seq3(256),
                pl.BlockSpec((1, DIFF_V), lambda n, s, pt: (0, 0))]
    args = [lam, qlat, qd, nckv, nkr, ndk, ndv, subln_g.reshape(1, DIFF_V)]
    for j in range(pps):
        page_map = lambda n, s, pt, j=j: (layer, pt[n, s * pps + j], 0, 0)
        in_specs += [pl.BlockSpec((1, 1, PAGE_SIZE, KV_LORA), page_map),
                     pl.BlockSpec((1, 1, MLA_ROPE, PAGE_SIZE), page_map),
                     pl.BlockSpec((1, 1, W2, 2 * DIFF_QK), page_map),
                     pl.BlockSpec((1, 1, W2, DIFF_V), page_map)]
        args += [cache_ckv, cache_krt, cache_dk2, cache_dv2]
    grid_spec = pltpu.PrefetchScalarGridSpec(
        num_scalar_prefetch=1,
        grid=(n_seq, n_pages // pps),
        in_specs=in_specs,
        out_specs=[pl.BlockSpec((1, R, KV_LORA), lambda n, s, pt: (n, 0, 0)),
                   pl.BlockSpec((1, DIFF_KV_HEADS, DIFF_GROUP, LANES), lambda n, s, pt: (n, 0, 0, 0))],
        scratch_shapes=[pltpu.VMEM((R, LANES), F32), pltpu.VMEM((R, LANES), F32), pltpu.VMEM((R, KV_LORA), F32),
                        pltpu.VMEM((RD, LANES), F32), pltpu.VMEM((RD, LANES), F32), pltpu.VMEM((RD, LANES), F32)],
    )
    return pl.pallas_call(
        functools.partial(_decode_kernel, pps=pps, past_len=past_len, lam_init=lam_init),
        grid_spec=grid_spec,
        out_shape=[jax.ShapeDtypeStruct((n_seq, R, KV_LORA), F32),
                   jax.ShapeDtypeStruct((n_seq, DIFF_KV_HEADS, DIFF_GROUP, LANES), F32)],
        compiler_params=_params("parallel", "arbitrary"),
        name="decode_attn",
    )(page_table, *args)


def _post_kernel(om_ref, od_ref, gates_ref, x_ref, mod_ref, wom_ref, wod_ref, wout_ref, gffn_ref, wpq_ref,
                 keys_ref, x1_ref, h2t_ref, st_ref):
    D = D_MODEL
    mod = mod_ref[0]
    gates = gates_ref[...]
    merged = gates[:, :D] * _dot(om_ref[...], wom_ref[...]) + gates[:, D:] * _dot(od_ref[...], wod_ref[...])
    x1 = x_ref[...] + mod[:, 2 * D:3 * D] * _dot(merged.astype(BF16), wout_ref[...])
    x1_ref[...] = x1
    h2 = _rms(x1, gffn_ref[...]) * (1.0 + mod[:, 4 * D:5 * D]) + mod[:, 3 * D:4 * D]
    h2t_ref[...] = h2.T.astype(BF16)
    pq = _dot(h2.astype(BF16), wpq_ref[...]).astype(BF16)
    for i in range(2 * PEER_HEADS):
        st_ref[i] = _dot_nt(keys_ref[i], pq[:, i * LANES:(i + 1) * LANES])


def _post(om, od, gates, x, mod, wom, wod, wout, g_ffn, wpq, keys, tm, tiles_per_row):
    t, D = x.shape
    mrows = mod.shape[1]
    tok = lambda w: pl.BlockSpec((tm, w), lambda i: (i, 0))
    const = lambda shape: pl.BlockSpec(shape, lambda i: (0,) * len(shape))
    mod_spec = (pl.BlockSpec((1, 1, 6 * D), lambda i: (i // tiles_per_row, 0, 0)) if mrows == 1
                else pl.BlockSpec((1, tm, 6 * D), lambda i: (0, i, 0)))
    nk = 2 * PEER_HEADS
    return pl.pallas_call(
        _post_kernel,
        grid=(t // tm,),
        in_specs=[tok(512), tok(1024), tok(2048), tok(D), mod_spec, const((512, D)), const((1024, D)),
                  const((D, D)), const((1, D)), const((D, PEER_HEADS * PEER_DKEY)),
                  const((nk, PEER_NKEYS, PEER_DKEY // 2))],
        out_specs=[tok(D), pl.BlockSpec((D, tm), lambda i: (0, i)),
                   pl.BlockSpec((nk, PEER_NKEYS, tm), lambda i: (0, 0, i))],
        out_shape=[jax.ShapeDtypeStruct((t, D), F32), jax.ShapeDtypeStruct((D, t), BF16),
                   jax.ShapeDtypeStruct((nk, PEER_NKEYS, t), F32)],
        compiler_params=_params("parallel"),
        name="post_attn",
    )(om, od, gates, x, mod, wom, wod, wout, g_ffn.reshape(1, D), wpq, keys)


N_EXT = PEER_TOPK + 1
CAND = [(i, j) for i in range(N_EXT) for j in range(N_EXT) if (i + 1) * (j + 1) <= N_EXT]


NOT_RANKED = 64.0


def _extract_desc(s, n, want_rank=False):
    out = []
    rank = jnp.full(s.shape, NOT_RANKED, F32) if want_rank else None
    for i in range(n):
        m = jnp.max(s, axis=0, keepdims=True)
        out.append(m)
        hit = s == m
        if want_rank:
            rank = jnp.where(hit, float(i), rank)
        s = jnp.where(hit, NEG_INF, s)
    return (out, rank) if want_rank else out


def _count_above(desc, x):
    assert len(desc) == 16
    above = lambda v: v > x
    pick = lambda c, a, b: jnp.where(c, a, b)
    c8 = above(desc[7])
    c4 = above(pick(c8, desc[11], desc[3]))
    c2 = above(pick(c8, pick(c4, desc[13], desc[9]), pick(c4, desc[5], desc[1])))
    c1 = above(pick(c8, pick(c4, pick(c2, desc[14], desc[12]), pick(c2, desc[10], desc[8])),
                    pick(c4, pick(c2, desc[6], desc[4]), pick(c2, desc[2], desc[0]))))
    n = (jnp.where(c8, 8.0, 0.0) + jnp.where(c4, 4.0, 0.0)) + (jnp.where(c2, 2.0, 0.0) + jnp.where(c1, 1.0, 0.0))
    return jnp.where(above(desc[15]), 16.0, n)


CAND_ROWS = -(-len(CAND) // 8) * 8


def _select_kernel(st_ref, r2_ref, e2_ref, jrow_ref, cc_ref, cand_ref):
    te = st_ref.shape[-1]
    cand_ref[...] = jnp.full(cand_ref.shape, NEG_INF, F32)

    def head(h, carry):
        s1 = st_ref[2 * h]
        s2 = st_ref[2 * h + 1]
        t1 = _extract_desc(s1, N_EXT)
        t2, rank2 = _extract_desc(s2, N_EXT, want_rank=True)
        for r, (i, j) in enumerate(CAND):
            cand_ref[r:r + 1, :] = t1[i] + t2[j]
        top = _extract_desc(cand_ref[...], N_EXT)
        z = jnp.ones((1, te), F32)
        for c in top[1:PEER_TOPK]:
            z = z + jnp.exp(c - top[0])
        tau = 0.5 * (top[PEER_TOPK - 1] + top[PEER_TOPK]) - s1
        jrow_ref[h] = _count_above(t2[:PEER_TOPK], tau)
        cc_ref[h] = jnp.exp(s1 - t1[0]) * (0.5 / z)
        r2_ref[h] = rank2.astype(BF16)
        e2_ref[h] = jnp.exp(s2 - t2[0]).astype(BF16)
        return carry

    lax.fori_loop(0, PEER_HEADS, head, 0)


def _peer_select(st, te):
    nk, n, t = st.shape
    spec = pl.BlockSpec((PEER_HEADS, n, te), lambda i: (0, 0, i))
    shape = lambda dt: jax.ShapeDtypeStruct((PEER_HEADS, n, t), dt)
    return pl.pallas_call(
        _select_kernel,
        grid=(t // te,),
        in_specs=[pl.BlockSpec((nk, n, te), lambda i: (0, 0, i))],
        out_specs=[spec, spec, spec, spec],
        out_shape=[shape(BF16), shape(BF16), shape(F32), shape(F32)],
        scratch_shapes=[pltpu.VMEM((CAND_ROWS, te), F32)],
        compiler_params=_params("parallel"),
        name="peer_select",
    )(st)


SUB = 8
PAIR = 2 * PEER_NKEYS


def _peer_kernel(h2t_ref, r2_ref, e2_ref, jrow_ref, cc_ref, pu_ref, pvt_ref, x1_ref, mod_ref, gfin_ref, y_ref,
                 acc_ref, ut_ref, wt_ref, r2s_ref, e2s_ref, *, ca):
    D = D_MODEL
    c = pl.program_id(1)
    tm = h2t_ref.shape[1]
    n_pair = ca // 2
    PK = 2 * SUB

    @pl.when(c == 0)
    def _():
        acc_ref[...] = jnp.zeros_like(acc_ref)
        r2s_ref[...] = r2_ref[...]
        e2s_ref[...] = e2_ref[...]

    def first_matmul(p):
        ut_ref[p % 2] = _dot(pu_ref[p * PAIR:(p + 1) * PAIR, :], h2t_ref[...])

    def gate_rows(p, ah):
        al = 2 * p + ah
        a8 = pl.multiple_of(c * ca + al // SUB * SUB, SUB)
        row_of = lambda ref, h, lanes: jnp.broadcast_to(
            ref[h, pl.ds(a8, SUB), lanes][al % SUB:al % SUB + 1], (PK, LANES)).astype(BF16)
        zero = jnp.zeros((PK, LANES), BF16)
        for tg in range(tm // LANES):
            lanes = slice(tg * LANES, (tg + 1) * LANES)
            js = [row_of(jrow_ref, h, lanes) for h in range(PEER_HEADS)]
            ccs = [row_of(cc_ref, h, lanes) for h in range(PEER_HEADS)]
            for bp in range(PEER_NKEYS // PK):
                rows = slice(PK * bp, PK * (bp + 1))
                g = None
                for h in range(PEER_HEADS):
                    t = jnp.where(r2s_ref[h, rows, lanes] < js[h], e2s_ref[h, rows, lanes] * ccs[h], zero)
                    g = t if g is None else g + t
                u0 = ah * PEER_NKEYS + PK * bp
                u = ut_ref[p % 2, u0:u0 + PK, lanes]
                act = (u * (1.0 + lax.erf(u * math.sqrt(0.5)))).astype(BF16)
                wt_ref[p * PAIR + u0:p * PAIR + u0 + PK, lanes] = g * act

    first_matmul(0)
    for p in range(n_pair):
        if p + 1 < n_pair:
            first_matmul(p + 1)
        gate_rows(p, 0)
        gate_rows(p, 1)
        if p % 2 == 1:
            lo, hi = (p - 1) * PAIR, (p + 1) * PAIR
            acc_ref[...] += _dot(pvt_ref[:, lo:hi], wt_ref[lo:hi, :])

    @pl.when(c == pl.num_programs(1) - 1)
    def _():
        mod = mod_ref[0]
        x2 = x1_ref[...] + mod[:, 5 * D:6 * D] * acc_ref[...].T
        y_ref[...] = _rms(x2, gfin_ref[...])


def _peer_dense(h2t, sel, pu, pvt, x1, mod, g_final, tm, ca, tiles_per_row):
    t, D = x1.shape
    mrows = mod.shape[1]
    nch = ca * PEER_NKEYS
    mod_spec = (pl.BlockSpec((1, 1, 6 * D), lambda i, c: (i // tiles_per_row, 0, 0)) if mrows == 1
                else pl.BlockSpec((1, tm, 6 * D), lambda i, c: (0, i, 0)))
    sel_spec = pl.BlockSpec((PEER_HEADS, PEER_NKEYS, tm), lambda i, c: (0, 0, i))
    return pl.pallas_call(
        functools.partial(_peer_kernel, ca=ca),
        grid=(t // tm, PEER_N // nch),
        in_specs=[pl.BlockSpec((D, tm), lambda i, c: (0, i)),
                  sel_spec, sel_spec, sel_spec, sel_spec,
                  pl.BlockSpec((nch, D), lambda i, c: (c, 0)),
                  pl.BlockSpec((D, nch), lambda i, c: (0, c)),
                  pl.BlockSpec((tm, D), lambda i, c: (i, 0)),
                  mod_spec,
                  pl.BlockSpec((1, D), lambda i, c: (0, 0))],
        out_specs=pl.BlockSpec((tm, D), lambda i, c: (i, 0)),
        out_shape=jax.ShapeDtypeStruct((t, D), F32),
        scratch_shapes=[pltpu.VMEM((D, tm), F32), pltpu.VMEM((2, PAIR, tm), F32), pltpu.VMEM((nch, tm), BF16),
                        pltpu.VMEM((PEER_HEADS, PEER_NKEYS, tm), BF16), pltpu.VMEM((PEER_HEADS, PEER_NKEYS, tm), BF16)],
        compiler_params=_params("parallel", "arbitrary"),
        name="peer_dense",
    )(h2t, *sel, pu, pvt, x1, mod, g_final.reshape(1, D))


def _rot_half(w):
    half = MLA_ROPE // 2
    return jnp.concatenate([-w[..., half:], w[..., :half]], axis=-1)


def _prep_layer(w_in, w_uq, w_uk, w_uv):
    D = D_MODEL
    o_kr = Q_LORA + KV_LORA
    w_kr = w_in[:, o_kr:o_kr + MLA_ROPE]
    zpad = jnp.zeros((D, LANES - MLA_ROPE), F32)
    win = jnp.concatenate([w_in[:, :o_kr], w_in[:, o_kr + MLA_ROPE:], w_kr, zpad, _rot_half(w_kr), zpad],
                          axis=1).astype(BF16)
    H = MLA_HEADS
    r = w_uq.reshape(Q_LORA, H, MLA_NOPE + MLA_ROPE)
    nope, rope = r[..., :MLA_NOPE], r[..., MLA_NOPE:]
    z32 = jnp.zeros((Q_LORA, H, LANES - MLA_NOPE - MLA_ROPE), F32)
    w_q = jnp.concatenate([nope, rope, z32], axis=-1).reshape(Q_LORA, H * LANES)
    w_qr = jnp.concatenate([jnp.zeros_like(nope), _rot_half(rope), z32], axis=-1).reshape(Q_LORA, H * LANES)
    wq = jnp.concatenate([w_q, w_qr], axis=1).astype(BF16)
    uk = w_uk.reshape(KV_LORA, H, MLA_NOPE)
    w_k = jnp.concatenate([uk, jnp.zeros_like(uk)], axis=-1).reshape(KV_LORA, H * LANES)
    uv = w_uv.reshape(KV_LORA, H // 2, 2, MLA_V)
    zv = jnp.zeros((KV_LORA, H // 2, MLA_V), F32)
    v_lo = jnp.concatenate([uv[:, :, 0], zv], axis=-1).reshape(KV_LORA, H // 2 * LANES)
    v_hi = jnp.concatenate([zv, uv[:, :, 1]], axis=-1).reshape(KV_LORA, H // 2 * LANES)
    wkv = jnp.concatenate([w_k, v_lo, v_hi], axis=1).astype(BF16)
    eye_h = jnp.eye(H, dtype=F32)
    uk_t = jnp.transpose(uk, (1, 2, 0))
    top = jnp.concatenate([uk_t, jnp.zeros((H, MLA_NOPE, LANES), F32)], axis=-1)
    mid = jnp.concatenate([jnp.zeros((H, MLA_ROPE, KV_LORA), F32),
                           jnp.broadcast_to(jnp.eye(MLA_ROPE, LANES, dtype=F32), (H, MLA_ROPE, LANES))], axis=-1)
    bot = jnp.zeros((H, LANES - MLA_NOPE - MLA_ROPE, KV_LORA + LANES), F32)
    m_h = jnp.concatenate([top, mid, bot], axis=1)
    w_abs = jnp.einsum('hrc,hg->hrgc', m_h, eye_h).reshape(H * LANES, H * (KV_LORA + LANES))
    uv_h = jnp.transpose(w_uv.reshape(KV_LORA, H, MLA_V), (1, 0, 2))
    w_uvbd = jnp.einsum('hcv,hg->hcgv', uv_h, eye_h).reshape(H * KV_LORA, H * MLA_V)
    return win, wq, wkv, w_abs, w_uvbd


def _rope_table(pos):
    inv = ROPE_THETA ** (-jnp.arange(0, MLA_ROPE, 2, dtype=F32) / MLA_ROPE)
    ang = pos.astype(F32)[:, None] * inv[None, :]
    cos, sin = jnp.cos(ang), jnp.sin(ang)
    n = pos.shape[0]
    z = lambda w: jnp.zeros((n, w), F32)
    c_q = MLA_SCALE * jnp.concatenate([jnp.ones((n, MLA_NOPE), F32), cos, cos, z(32)], axis=1)
    s_q = MLA_SCALE * jnp.concatenate([z(MLA_NOPE), sin, sin, z(32)], axis=1)
    c_k = jnp.concatenate([cos, cos, z(96)], axis=1)
    s_k = jnp.concatenate([sin, sin, z(96)], axis=1)
    return jnp.concatenate([c_q, s_q, c_k, s_k], axis=1)


def _pick(n, prefs):
    for p in prefs:
        if n % p == 0:
            return p
    return n


def kernel(x_prompt, x_sample, c_prompt, c_sample, cache_ckv, cache_kr, cache_dk, cache_dv, page_table, w_ada, b_ada, norm_attn_g, norm_ffn_g, w_in, mla_qnorm_g, w_uq, mla_kvnorm_g, w_uk, w_uv, w_o_mla, diff_lq1, diff_lk1, diff_lq2, diff_lk2, diff_subln_g, w_o_diff, w_out, w_pq, peer_keys, peer_u, peer_v, norm_final_g):
    D = D_MODEL
    nb, s_len, _ = x_prompt.shape
    n_seq, dec_seq, _ = x_sample.shape
    assert dec_seq == 1
    depth = w_ada.shape[0]
    n_pages = page_table.shape[1]
    past_len = n_pages * PAGE_SIZE
    t_p = nb * s_len

    tab_p = _rope_table(jnp.arange(s_len))
    tab_s = _rope_table(jnp.full((n_seq,), past_len, jnp.int32))
    tm_in = _pick(s_len, (256, 128))
    tq = _pick(s_len, (256, 128))
    tm_post = _pick(s_len, (256, 128))
    te = _pick(s_len, (256, 128))
    tm_peer = _pick(s_len, (512, 256, 128))
    ca = 16
    pps = _pick(n_pages, (16, 8, 4, 2, 1))

    xp = x_prompt
    xs = x_sample.reshape(1, n_seq, D)
    c_all = jnp.concatenate([c_prompt, c_sample], axis=0)
    new_p, new_s = ([], [], [], []), ([], [], [], [])
    yp = ys = None
    for l in range(depth):
        lam_init = 0.8 - 0.6 * math.exp(-0.3 * l)
        lam = (jnp.exp(jnp.sum(diff_lq1[l] * diff_lk1[l])) - jnp.exp(jnp.sum(diff_lq2[l] * diff_lk2[l]))
               + lam_init).astype(F32).reshape(1)
        win, wq, wkv, w_abs, w_uvbd = _prep_layer(w_in[l], w_uq[l], w_uk[l], w_uv[l])
        wom, wod, wout = w_o_mla[l].astype(BF16), w_o_diff[l].astype(BF16), w_out[l].astype(BF16)
        wpq = w_pq[l].astype(BF16)
        keys = peer_keys[l].reshape(2 * PEER_HEADS, PEER_NKEYS, PEER_DKEY // 2).astype(BF16)
        pu = peer_u[l].astype(BF16)
        pvt = peer_v[l].T.astype(BF16)
        last = l == depth - 1
        g_fin = norm_final_g if last else jnp.ones((D,), F32)

        mod = _small_mm(c_all, w_ada[l], b_ada[l], pre_silu=True, tn=1024)
        mod_p = mod[:nb].reshape(nb, 1, 6 * D)
        mod_s = mod[nb:].reshape(1, n_seq, 6 * D)

        def mixer(x, modx, tab, tm):
            return _inproj(x, modx, tab, norm_attn_g[l], win, mla_qnorm_g[l], wq, mla_kvnorm_g[l], wkv, tm)

        def ffn(om, od, gates, x2d, modx, tm_a, tm_b, te_, tiles_per_row):
            x1, h2t, st = _post(om, od, gates, x2d, modx, wom, wod, wout, norm_ffn_g[l], wpq, keys, tm_a,
                                max(tiles_per_row * tm_b // tm_a, 1))
            sel = _peer_select(st, te_)
            return _peer_dense(h2t, sel, pu, pvt, x1, modx, g_fin, tm_b, ca, tiles_per_row)

        (ckv, kr, dk, dv, q, k, vlo, vhi, dq, dkb, dvb, gates) = mixer(xp, mod_p, tab_p, tm_in)
        om, od = _prompt_attention(lam, q, k, vlo, vhi, dq, dkb, dvb, diff_subln_g[l], lam_init, tq)
        yp = ffn(om.reshape(t_p, 512), od.reshape(t_p, 1024), gates.reshape(t_p, 2048), xp.reshape(t_p, D),
                 mod_p, tm_post, tm_peer, te, s_len // tm_peer)
        for lst, r in zip(new_p, (ckv, kr, dk.reshape(nb, s_len, DIFF_KV_HEADS, 2 * DIFF_QK),
                                  dv.reshape(nb, s_len, DIFF_KV_HEADS, DIFF_V))):
            lst.append(r)

        (ckv_s, kr_s, dk_s, dv_s, q_s, _, _, _, dq_s, _, _, gates_s) = mixer(xs, mod_s, tab_s, n_seq)
        qlat = _small_mm(q_s[0].astype(F32), w_abs, tn=1024).reshape(n_seq, MLA_HEADS, KV_LORA + LANES)
        dq4 = dq_s[0].astype(F32).reshape(n_seq, DIFF_KV_HEADS, DIFF_GROUP, 2, DIFF_QK)
        zq = jnp.zeros_like(dq4[:, :, :, 0])
        qd = jnp.concatenate([jnp.concatenate([dq4[:, :, :, 0], zq], axis=-1),
                              jnp.concatenate([zq, dq4[:, :, :, 1]], axis=-1)],
                             axis=2).reshape(n_seq, DIFF_KV_HEADS * MLA_HEADS, LANES)
        olat, od_s = _decode_attention(
            page_table, lam, qlat, qd, ckv_s.reshape(n_seq, 1, KV_LORA), kr_s.reshape(n_seq, 1, MLA_ROPE),
            dk_s.reshape(n_seq, 1, 256), dv_s.reshape(n_seq, 1, 256), diff_subln_g[l],
            cache_ckv, cache_kr, cache_dk, cache_dv, l, lam_init, pps)
        om_s = _small_mm(olat.reshape(n_seq, MLA_HEADS * KV_LORA), w_uvbd).astype(BF16)
        ys = ffn(om_s, od_s.reshape(n_seq, 1024).astype(BF16), gates_s[0], xs[0], mod_s, n_seq, n_seq, n_seq, 1)
        for lst, r in zip(new_s, (ckv_s.reshape(n_seq, 1, KV_LORA), kr_s.reshape(n_seq, 1, MLA_ROPE),
                                  dk_s.reshape(n_seq, 1, DIFF_KV_HEADS, 2 * DIFF_QK),
                                  dv_s.reshape(n_seq, 1, DIFF_KV_HEADS, DIFF_V))):
            lst.append(r)
        if not last:
            raise NotImplementedError("multi-layer stacking needs the un-normalised residual stream")

    y_prompt = yp.reshape(nb, s_len, D)
    y_sample = ys.reshape(n_seq, 1, D)
    stack = lambda t: jnp.stack(t, axis=0)
    return (y_prompt, y_sample, *[stack(t) for t in new_p], *[stack(t) for t in new_s])
```

```python
import functools
import math

import jax
import jax.numpy as jnp
from jax import lax
from jax.experimental import pallas as pl
from jax.experimental.pallas import tpu as pltpu

D_MODEL = 1024
PAGE_SIZE = 128
MLA_HEADS = 8
MLA_NOPE = 64
MLA_ROPE = 32
MLA_V = 64
Q_LORA = 384
KV_LORA = 256
ROPE_THETA = 10000.0
MLA_SCALE = (MLA_NOPE + MLA_ROPE) ** -0.5
DIFF_HEADS = 8
DIFF_KV_HEADS = 2
DIFF_GROUP = DIFF_HEADS // DIFF_KV_HEADS
DIFF_QK = 64
DIFF_V = 2 * DIFF_QK
PEER_HEADS = 8
PEER_NKEYS = 128
PEER_N = PEER_NKEYS * PEER_NKEYS
PEER_DKEY = 256
PEER_TOPK = 16
RMS_EPS = 1e-6

LANES = 128
VMEM_LIMIT = 56 * 1024 * 1024
IN_PAD = 4480
NEG_INF = float("-inf")

BF16 = jnp.bfloat16
F32 = jnp.float32


def _params(*sem):
    return pltpu.CompilerParams(dimension_semantics=sem, vmem_limit_bytes=VMEM_LIMIT)


def _dot(a, b):
    return jnp.dot(a, b, preferred_element_type=F32)


def _dot_nt(a, b):
    return lax.dot_general(a, b, (((1,), (1,)), ((), ())), preferred_element_type=F32)


def _rms(x, g):
    return x * lax.rsqrt(jnp.mean(x * x, axis=-1, keepdims=True) + RMS_EPS) * g


def _small_mm_kernel(a_ref, w_ref, b_ref, o_ref, *, pre_silu):
    a = a_ref[...]
    if pre_silu:
        a = a * jax.nn.sigmoid(a)
    o_ref[...] = _dot(a.astype(BF16), w_ref[...].astype(BF16)) + b_ref[...]


def _small_mm(a, w, bias=None, pre_silu=False, tn=512):
    m, k = a.shape
    n = w.shape[1]
    tn = min(tn, n)
    if bias is None:
        bias = jnp.zeros((n,), F32)
    return pl.pallas_call(
        functools.partial(_small_mm_kernel, pre_silu=pre_silu),
        grid=(n // tn,),
        in_specs=[pl.BlockSpec((m, k), lambda j: (0, 0)),
                  pl.BlockSpec((k, tn), lambda j: (0, j)),
                  pl.BlockSpec((1, tn), lambda j: (0, j))],
        out_specs=pl.BlockSpec((m, tn), lambda j: (0, j)),
        out_shape=jax.ShapeDtypeStruct((m, n), F32),
        compiler_params=_params("arbitrary"),
        name="small_mm",
    )(a, w, bias.reshape(1, n))


C_CQ, C_CKV, C_DQ, C_DK, C_DV, C_GA, C_GB, C_KR, C_KRR = 0, 384, 640, 1664, 1920, 2176, 3200, 4224, 4352


def _inproj_kernel(x_ref, mod_ref, tab_ref, gattn_ref, win_ref, qng_ref, wq_ref, kvg_ref, wkv_ref,
                   ckv_ref, kr_ref, dk_ref, dv_ref, q_ref, k_ref, vlo_ref, vhi_ref, dq_ref,
                   dkb_ref, dvb_ref, gates_ref):
    D = D_MODEL
    x = x_ref[0]
    mod = mod_ref[0]
    h = _rms(x, gattn_ref[...]) * (1.0 + mod[:, D:2 * D]) + mod[:, 0:D]
    hb = h.astype(BF16)
    tab = tab_ref[...]
    c_q, s_q, c_k, s_k = (tab[:, i * LANES:(i + 1) * LANES] for i in range(4))

    def proj(lo, hi):
        return _dot(hb, win_ref[:, lo:hi])

    gates_ref[0] = jax.nn.sigmoid(proj(C_GA, C_KR))
    dq_ref[0] = (proj(C_DQ, C_DK) * (DIFF_QK ** -0.5)).astype(BF16)
    dk = proj(C_DK, C_DV)
    dv = proj(C_DV, C_GA)
    dk_ref[0] = dk
    dv_ref[0] = dv
    dkb_ref[0] = dk.astype(BF16)
    dvb_ref[0] = dv.astype(BF16)

    kr2 = proj(C_KR, IN_PAD)
    kr = kr2[:, :LANES] * c_k + kr2[:, LANES:] * s_k
    kr_ref[0] = kr[:, :MLA_ROPE]
    kr_sh = pltpu.roll(kr, MLA_NOPE, 1)

    ckv = _rms(proj(C_CKV, C_DQ), kvg_ref[...])
    ckv_ref[0] = ckv
    kv = _dot(ckv.astype(BF16), wkv_ref[...])
    for hd in range(MLA_HEADS):
        sl = slice(hd * LANES, (hd + 1) * LANES)
        k_ref[0, :, sl] = (kv[:, sl] + kr_sh).astype(BF16)
    vlo_ref[0] = kv[:, 1024:1536].astype(BF16)
    vhi_ref[0] = kv[:, 1536:2048].astype(BF16)

    cq = _rms(proj(C_CQ, C_CKV), qng_ref[...]).astype(BF16)
    q2 = _dot(cq, wq_ref[...])
    for hd in range(MLA_HEADS):
        sl = slice(hd * LANES, (hd + 1) * LANES)
        sr = slice(1024 + hd * LANES, 1024 + (hd + 1) * LANES)
        q_ref[0, :, sl] = (q2[:, sl] * c_q + q2[:, sr] * s_q).astype(BF16)


def _inproj(x, mod, tab, g_attn, win, qn_g, wq, kvn_g, wkv, tm):
    nb, s, D = x.shape
    mrows = mod.shape[1]
    grid = (nb, s // tm)
    tok = lambda w, dt: jax.ShapeDtypeStruct((nb, s, w), dt)
    tspec = lambda w: pl.BlockSpec((1, tm, w), lambda b, i: (b, i, 0))
    const = lambda shape: pl.BlockSpec(shape, lambda b, i: (0,) * len(shape))
    mod_spec = (pl.BlockSpec((1, 1, 6 * D), lambda b, i: (b, 0, 0)) if mrows == 1
                else pl.BlockSpec((1, tm, 6 * D), lambda b, i: (b, i, 0)))
    widths = [(KV_LORA, F32), (MLA_ROPE, F32), (256, F32), (256, F32), (1024, BF16), (1024, BF16),
              (512, BF16), (512, BF16), (1024, BF16), (256, BF16), (256, BF16), (2048, F32)]
    return pl.pallas_call(
        _inproj_kernel,
        grid=grid,
        in_specs=[tspec(D), mod_spec, pl.BlockSpec((tm, 4 * LANES), lambda b, i: (i, 0)),
                  const((1, D)), const((D, IN_PAD)), const((1, Q_LORA)), const((Q_LORA, 2048)),
                  const((1, KV_LORA)), const((KV_LORA, 2048))],
        out_specs=[tspec(w) for w, _ in widths],
        out_shape=[tok(w, dt) for w, dt in widths],
        compiler_params=_params("parallel", "arbitrary"),
        name="inproj",
    )(x, mod, tab, g_attn.reshape(1, D), win, qn_g.reshape(1, Q_LORA), wq, kvn_g.reshape(1, KV_LORA), wkv)


N_CHAIN = MLA_HEADS + 2 * DIFF_HEADS
CHAIN_GROUPS = ((tuple(range(MLA_HEADS)), tuple(range(DIFF_HEADS))),)


def _softmax_update(s, v, m_ref, l_ref, acc_ref):
    reps = s.shape[1] // LANES
    m_prev = m_ref[...]
    m_next = jnp.maximum(m_prev, jnp.max(s, axis=1, keepdims=True))
    p = jnp.exp(s - jnp.tile(m_next, (1, reps)))
    alpha = jnp.exp(m_prev - m_next)
    l_ref[...] = alpha * l_ref[...] + jnp.sum(p, axis=1, keepdims=True)
    m_ref[...] = m_next
    acc_ref[...] = alpha * acc_ref[...] + _dot(p.astype(BF16), v)


def _attn_kernel(lam_ref, q_ref, k_ref, vlo_ref, vhi_ref, dq_ref, dk_ref, dv_ref, subg_ref,
                 om_ref, od_ref, qd_ref, m_ref, l_ref, acc_ref, *, tq, tk, lam_init):
    q0 = pl.program_id(1) * tq
    n_full = q0 // tk
    lam = lam_ref[0]

    m_ref[...] = jnp.full(m_ref.shape, NEG_INF, F32)
    l_ref[...] = jnp.zeros(l_ref.shape, F32)
    acc_ref[...] = jnp.zeros(acc_ref.shape, F32)
    lane = lax.broadcasted_iota(jnp.int32, (tq, LANES), 1)
    for hd in range(DIFF_HEADS):
        qd = dq_ref[0, :, hd * LANES:(hd + 1) * LANES]
        zero = jnp.zeros_like(qd)
        qd_ref[2 * hd] = jnp.where(lane < DIFF_QK, qd, zero)
        qd_ref[2 * hd + 1] = jnp.where(lane >= DIFF_QK, qd, zero)

    col = lax.broadcasted_iota(jnp.int32, (tq, tk), 1)
    row = lax.broadcasted_iota(jnp.int32, (tq, tk), 0)

    def chunk(j, diagonal, mla_heads, diff_heads):
        off = pl.multiple_of(j * tk, tk)
        rows = lambda ref, lane0: ref[0, pl.ds(off, tk), lane0:lane0 + LANES]
        rel = col - row + (off - q0)
        for hd in mla_heads:
            s = _dot_nt(q_ref[0, :, hd * LANES:(hd + 1) * LANES], rows(k_ref, hd * LANES))
            if diagonal:
                s = jnp.where(rel <= 0, s, NEG_INF)
            v = rows(vlo_ref if hd % 2 == 0 else vhi_ref, (hd // 2) * LANES)
            _softmax_update(s, v, m_ref.at[hd], l_ref.at[hd], acc_ref.at[hd])
        dist = rel.astype(F32)
        if diagonal:
            dist = jnp.where(rel <= 0, dist, NEG_INF)
        for hd in diff_heads:
            kv = hd // DIFF_GROUP
            bias = dist * (2.0 ** (-8.0 * (hd + 1) / DIFF_HEADS))
            kk = rows(dk_ref, kv * LANES)
            vv = rows(dv_ref, kv * LANES)
            for mp in range(2):
                c = MLA_HEADS + 2 * hd + mp
                s = _dot_nt(qd_ref[2 * hd + mp], kk) + bias
                _softmax_update(s, vv, m_ref.at[c], l_ref.at[c], acc_ref.at[c])

    for mla_heads, diff_heads in CHAIN_GROUPS:
        def body(j, carry, mla_heads=mla_heads, diff_heads=diff_heads):
            chunk(j, False, mla_heads, diff_heads)
            return carry

        lax.fori_loop(0, n_full, body, 0)
        chunk(n_full, True, mla_heads, diff_heads)

    inv = lambda c: acc_ref[c] / l_ref[c]
    for pair in range(MLA_HEADS // 2):
        om_ref[0, :, pair * LANES:(pair + 1) * LANES] = (inv(2 * pair) + inv(2 * pair + 1)).astype(BF16)
    for hd in range(DIFF_HEADS):
        c = MLA_HEADS + 2 * hd
        o = inv(c) - lam * inv(c + 1)
        od_ref[0, :, hd * LANES:(hd + 1) * LANES] = (_rms(o, subg_ref[...]) * (1.0 - lam_init)).astype(BF16)


def _prompt_attention(lam, q, k, vlo, vhi, dq, dkb, dvb, subln_g, lam_init, tq):
    nb, s, _ = q.shape
    qspec = lambda w: pl.BlockSpec((1, tq, w), lambda b, i: (b, i, 0))
    full = lambda w: pl.BlockSpec((1, s, w), lambda b, i: (b, 0, 0))
    state = pltpu.VMEM((N_CHAIN, tq, LANES), F32)
    return pl.pallas_call(
        functools.partial(_attn_kernel, tq=tq, tk=tq, lam_init=lam_init),
        grid=(nb, s // tq),
        in_specs=[pl.BlockSpec(memory_space=pltpu.SMEM), qspec(1024), full(1024), full(512), full(512),
                  qspec(1024), full(256), full(256), pl.BlockSpec((1, DIFF_V), lambda b, i: (0, 0))],
        out_specs=[qspec(512), qspec(1024)],
        out_shape=[jax.ShapeDtypeStruct((nb, s, 512), BF16), jax.ShapeDtypeStruct((nb, s, 1024), BF16)],
        scratch_shapes=[pltpu.VMEM((2 * DIFF_HEADS, tq, LANES), BF16), state, state, state],
        compiler_params=_params("parallel", "arbitrary"),
        name="prompt_attn",
    )(lam, q, k, vlo, vhi, dq, dkb, dvb, subln_g.reshape(1, DIFF_V))


def _decode_kernel(pt_ref, lam_ref, qlat_ref, qd_ref, nckv_ref, nkr_ref, ndk_ref, ndv_ref, subg_ref, *refs,
                   pps, past_len, lam_init):
    pages = refs[:4 * pps]
    olat_ref, od_ref = refs[4 * pps], refs[4 * pps + 1]
    m_m, l_m, a_m, m_d, l_d, a_d = refs[4 * pps + 2:]
    step = pl.program_id(1)
    n_steps = pl.num_programs(1)
    R = MLA_HEADS
    RD = DIFF_KV_HEADS * R
    W2 = DIFF_KV_HEADS * PAGE_SIZE

    qlat = qlat_ref[0]
    qd = qd_ref[0]
    kv_row = lax.broadcasted_iota(jnp.int32, (RD, LANES), 0) // R

    @pl.when(step == 0)
    def _():
        s = (jnp.sum(qlat[:, :KV_LORA] * nckv_ref[0], axis=-1, keepdims=True)
             + jnp.sum(qlat[:, KV_LORA:KV_LORA + MLA_ROPE] * nkr_ref[0], axis=-1, keepdims=True))
        m_m[...] = jnp.broadcast_to(s, (R, LANES))
        l_m[...] = jnp.ones((R, LANES), F32)
        a_m[...] = jnp.broadcast_to(nckv_ref[0], (R, KV_LORA))
        ndk, ndv = ndk_ref[0], ndv_ref[0]
        own = lambda t: jnp.where(kv_row == 0, jnp.broadcast_to(t[:, :LANES], (RD, LANES)),
                                  jnp.broadcast_to(t[:, LANES:], (RD, LANES)))
        m_d[...] = jnp.broadcast_to(jnp.sum(qd * own(ndk), axis=-1, keepdims=True), (RD, LANES))
        l_d[...] = jnp.ones((RD, LANES), F32)
        a_d[...] = own(ndv)

    def update(s_list, v_list, m_ref, l_ref, a_ref):
        m_old = m_ref[...]
        mx = s_list[0]
        for s in s_list[1:]:
            mx = jnp.maximum(mx, s)
        m_new = jnp.maximum(m_old, jnp.max(mx, axis=-1, keepdims=True))
        alpha = jnp.exp(m_old - m_new)
        reps = s_list[0].shape[1] // LANES
        m_b = jnp.tile(m_new, (1, reps))
        psum = None
        pv = None
        for s, v in zip(s_list, v_list):
            p = jnp.exp(s - m_b)
            psum = p if psum is None else psum + p
            d = _dot(p.astype(BF16), v)
            pv = d if pv is None else pv + d
        m_ref[...] = m_new
        l_ref[...] = alpha * l_ref[...] + jnp.sum(psum, axis=-1, keepdims=True)
        a_ref[...] = jnp.tile(alpha, (1, a_ref.shape[1] // LANES)) * a_ref[...] + pv

    qc = qlat[:, :KV_LORA].astype(BF16)
    qr = qlat[:, KV_LORA:KV_LORA + MLA_ROPE].astype(BF16)
    ckvs = [pages[4 * j][0, 0].astype(BF16) for j in range(pps)]
    s_m = [_dot_nt(qc, ckvs[j]) + _dot(qr, pages[4 * j + 1][0, 0].astype(BF16)) for j in range(pps)]
    update(s_m, ckvs, m_m, l_m, a_m)

    col = lax.broadcasted_iota(jnp.int32, (RD, W2), 1)
    row = lax.broadcasted_iota(jnp.int32, (RD, W2), 0)
    hd = (row // R) * DIFF_GROUP + row % DIFF_GROUP
    slope = jnp.exp2(-8.0 * (hd + 1).astype(F32) / DIFF_HEADS)
    base = jnp.where(col % DIFF_KV_HEADS == row // R, slope * (col // DIFF_KV_HEADS).astype(F32), NEG_INF)
    qdb = qd.astype(BF16)
    s_d, v_d = [], []
    for j in range(pps):
        dist0 = ((step * pps + j) * PAGE_SIZE - past_len).astype(F32)
        s_d.append(_dot_nt(qdb, pages[4 * j + 2][0, 0].astype(BF16)) + (base + slope * dist0))
        v_d.append(pages[4 * j + 3][0, 0].astype(BF16))
    update(s_d, v_d, m_d, l_d, a_d)

    @pl.when(step == n_steps - 1)
    def _():
        olat_ref[0] = a_m[...] / jnp.tile(l_m[...], (1, KV_LORA // LANES))
        lam = lam_ref[0]
        o2 = a_d[...] / l_d[...]
        for kv in range(DIFF_KV_HEADS):
            o = o2[kv * R:kv * R + DIFF_GROUP] - lam * o2[kv * R + DIFF_GROUP:(kv + 1) * R]
            od_ref[0, kv] = _rms(o, subg_ref[...]) * (1.0 - lam_init)


def _decode_attention(page_table, lam, qlat, qd, nckv, nkr, ndk, ndv, subln_g, cache_ckv, cache_kr,
                      cache_dk, cache_dv, layer, lam_init, pps):
    n_seq, n_pages = page_table.shape
    past_len = n_pages * PAGE_SIZE
    R = MLA_HEADS
    RD = DIFF_KV_HEADS * R
    W2 = DIFF_KV_HEADS * PAGE_SIZE
    n_layers, n_pool = cache_ckv.shape[:2]
    cache_krt = jnp.swapaxes(cache_kr, 2, 3)
    cache_dk2 = cache_dk.reshape(n_layers, n_pool, W2, 2 * DIFF_QK)
    cache_dv2 = cache_dv.reshape(n_layers, n_pool, W2, DIFF_V)
    seq3 = lambda w: pl.BlockSpec((1, 1, w), lambda n, s, pt: (n, 0, 0))
    in_specs = [pl.BlockSpec(memory_space=pltpu.SMEM),
                pl.BlockSpec((1, R, 384), lambda n, s, pt: (n, 0, 0)),
                pl.BlockSpec((1, RD, LANES), lambda n, s, pt: (n, 0, 0)),
                seq3(KV_LORA), seq3(MLA_ROPE), seq3(256), seq3(256),
                pl.BlockSpec((1, DIFF_V), lambda n, s, pt: (0, 0))]
    args = [lam, qlat, qd, nckv, nkr, ndk, ndv, subln_g.reshape(1, DIFF_V)]
    for j in range(pps):
        page_map = lambda n, s, pt, j=j: (layer, pt[n, s * pps + j], 0, 0)
        in_specs += [pl.BlockSpec((1, 1, PAGE_SIZE, KV_LORA), page_map),
                     pl.BlockSpec((1, 1, MLA_ROPE, PAGE_SIZE), page_map),
                     pl.BlockSpec((1, 1, W2, 2 * DIFF_QK), page_map),
                     pl.BlockSpec((1, 1, W2, DIFF_V), page_map)]
        args += [cache_ckv, cache_krt, cache_dk2, cache_dv2]
    grid_spec = pltpu.PrefetchScalarGridSpec(
        num_scalar_prefetch=1,
        grid=(n_seq, n_pages // pps),
        in_specs=in_specs,
        out_specs=[pl.BlockSpec((1, R, KV_LORA), lambda n, s, pt: (n, 0, 0)),
                   pl.BlockSpec((1, DIFF_KV_HEADS, DIFF_GROUP, LANES), lambda n, s, pt: (n, 0, 0, 0))],
        scratch_shapes=[pltpu.VMEM((R, LANES), F32), pltpu.VMEM((R, LANES), F32), pltpu.VMEM((R, KV_LORA), F32),
                        pltpu.VMEM((RD, LANES), F32), pltpu.VMEM((RD, LANES), F32), pltpu.VMEM((RD, LANES), F32)],
    )
    return pl.pallas_call(
        functools.partial(_decode_kernel, pps=pps, past_len=past_len, lam_init=lam_init),
        grid_spec=grid_spec,
        out_shape=[jax.ShapeDtypeStruct((n_seq, R, KV_LORA), F32),
                   jax.ShapeDtypeStruct((n_seq, DIFF_KV_HEADS, DIFF_GROUP, LANES), F32)],
        compiler_params=_params("parallel", "arbitrary"),
        name="decode_attn",
    )(page_table, *args)


def _post_kernel(om_ref, od_ref, gates_ref, x_ref, mod_ref, wom_ref, wod_ref, wout_ref, gffn_ref, wpq_ref,
                 keys_ref, x1_ref, h2t_ref, st_ref):
    D = D_MODEL
    mod = mod_ref[0]
    gates = gates_ref[...]
    merged = gates[:, :D] * _dot(om_ref[...], wom_ref[...]) + gates[:, D:] * _dot(od_ref[...], wod_ref[...])
    x1 = x_ref[...] + mod[:, 2 * D:3 * D] * _dot(merged.astype(BF16), wout_ref[...])
    x1_ref[...] = x1
    h2 = _rms(x1, gffn_ref[...]) * (1.0 + mod[:, 4 * D:5 * D]) + mod[:, 3 * D:4 * D]
    h2t_ref[...] = h2.T.astype(BF16)
    pq = _dot(h2.astype(BF16), wpq_ref[...]).astype(BF16)
    for i in range(2 * PEER_HEADS):
        st_ref[i] = _dot_nt(keys_ref[i], pq[:, i * LANES:(i + 1) * LANES])


def _post(om, od, gates, x, mod, wom, wod, wout, g_ffn, wpq, keys, tm, tiles_per_row):
    t, D = x.shape
    mrows = mod.shape[1]
    tok = lambda w: pl.BlockSpec((tm, w), lambda i: (i, 0))
    const = lambda shape: pl.BlockSpec(shape, lambda i: (0,) * len(shape))
    mod_spec = (pl.BlockSpec((1, 1, 6 * D), lambda i: (i // tiles_per_row, 0, 0)) if mrows == 1
                else pl.BlockSpec((1, tm, 6 * D), lambda i: (0, i, 0)))
    nk = 2 * PEER_HEADS
    return pl.pallas_call(
        _post_kernel,
        grid=(t // tm,),
        in_specs=[tok(512), tok(1024), tok(2048), tok(D), mod_spec, const((512, D)), const((1024, D)),
                  const((D, D)), const((1, D)), const((D, PEER_HEADS * PEER_DKEY)),
                  const((nk, PEER_NKEYS, PEER_DKEY // 2))],
        out_specs=[tok(D), pl.BlockSpec((D, tm), lambda i: (0, i)),
                   pl.BlockSpec((nk, PEER_NKEYS, tm), lambda i: (0, 0, i))],
        out_shape=[jax.ShapeDtypeStruct((t, D), F32), jax.ShapeDtypeStruct((D, t), BF16),
                   jax.ShapeDtypeStruct((nk, PEER_NKEYS, t), F32)],
        compiler_params=_params("parallel"),
        name="post_attn",
    )(om, od, gates, x, mod, wom, wod, wout, g_ffn.reshape(1, D), wpq, keys)


N_EXT = PEER_TOPK + 1
CAND = [(i, j) for i in range(N_EXT) for j in range(N_EXT) if (i + 1) * (j + 1) <= N_EXT]


def _extract_desc(s, n):
    out = []
    for _ in range(n):
        m = jnp.max(s, axis=0, keepdims=True)
        out.append(m)
        s = jnp.where(s == m, NEG_INF, s)
    return out


CAND_ROWS = -(-len(CAND) // 8) * 8
PEER_CA = 16


def _select_kernel(st_ref, e2_ref, te_ref, cc_ref, cand_ref):
    te = st_ref.shape[-1]
    cand_ref[...] = jnp.full(cand_ref.shape, NEG_INF, F32)

    def head(h, carry):
        s1 = st_ref[2 * h]
        s2 = st_ref[2 * h + 1]
        t1 = _extract_desc(s1, N_EXT)
        t2 = _extract_desc(s2, N_EXT)
        for r, (i, j) in enumerate(CAND):
            cand_ref[r:r + 1, :] = t1[i] + t2[j]
        top = _extract_desc(cand_ref[...], N_EXT)
        z = jnp.ones((1, te), F32)
        for c in top[1:PEER_TOPK]:
            z = z + jnp.exp(c - top[0])
        e2_ref[h] = jnp.exp(s2 - t2[0])
        thr_e = jnp.exp((0.5 * (top[PEER_TOPK - 1] + top[PEER_TOPK]) - s1) - t2[0])
        cc = jnp.exp(s1 - t1[0]) * (0.5 / z)
        for k in range(PEER_NKEYS // PEER_CA):
            te_ref[h, k] = thr_e[k * PEER_CA:(k + 1) * PEER_CA]
            cc_ref[h, k] = cc[k * PEER_CA:(k + 1) * PEER_CA]
        return carry

    lax.fori_loop(0, PEER_HEADS, head, 0)


def _peer_select(st, te):
    nk, n, t = st.shape
    nck = n // PEER_CA
    spec4 = pl.BlockSpec((PEER_HEADS, nck, PEER_CA, te), lambda i: (0, 0, 0, i))
    shape4 = jax.ShapeDtypeStruct((PEER_HEADS, nck, PEER_CA, t), F32)
    return pl.pallas_call(
        _select_kernel,
        grid=(t // te,),
        in_specs=[pl.BlockSpec((nk, n, te), lambda i: (0, 0, i))],
        out_specs=[pl.BlockSpec((PEER_HEADS, n, te), lambda i: (0, 0, i)), spec4, spec4],
        out_shape=[jax.ShapeDtypeStruct((PEER_HEADS, n, t), F32), shape4, shape4],
        scratch_shapes=[pltpu.VMEM((CAND_ROWS, te), F32)],
        compiler_params=_params("parallel"),
        name="peer_select",
    )(st)


SUB = 8
PAIR = 2 * PEER_NKEYS


def _peer_kernel(h2t_ref, e2_ref, te_ref, cc_ref, pu_ref, pvt_ref, x1_ref, mod_ref, gfin_ref, y_ref,
                 acc_ref, ut_ref, wt_ref, *, ca):
    D = D_MODEL
    c = pl.program_id(1)
    tm = h2t_ref.shape[1]
    n_pair = ca // 2

    @pl.when(c == 0)
    def _():
        acc_ref[...] = jnp.zeros_like(acc_ref)

    def first_matmul(p):
        ut_ref[p % 2] = _dot(pu_ref[p * PAIR:(p + 1) * PAIR, :], h2t_ref[...])

    def gate_rows(p, ah):
        al = 2 * p + ah
        row_of = lambda ref, h, lanes: jnp.broadcast_to(ref[h, c, al:al + 1, lanes], (SUB, LANES))
        for tg in range(tm // LANES):
            lanes = slice(tg * LANES, (tg + 1) * LANES)
            tes = [row_of(te_ref, h, lanes) for h in range(PEER_HEADS)]
            ccs = [row_of(cc_ref, h, lanes) for h in range(PEER_HEADS)]
            for bp in range(PEER_NKEYS // (2 * SUB)):
                ws = []
                for r0 in (2 * SUB * bp, 2 * SUB * bp + SUB):
                    g = None
                    for h in range(PEER_HEADS):
                        e2 = e2_ref[h, r0:r0 + SUB, lanes]
                        t = jnp.where(e2 > tes[h], e2 * ccs[h], 0.0)
                        g = t if g is None else g + t
                    u = ut_ref[p % 2, ah * PEER_NKEYS + r0:ah * PEER_NKEYS + r0 + SUB, lanes]
                    ws.append((g * u) * (1.0 + lax.erf(u * math.sqrt(0.5))))
                w0 = p * PAIR + ah * PEER_NKEYS + 2 * SUB * bp
                wt_ref[w0:w0 + 2 * SUB, lanes] = jnp.concatenate(ws, axis=0).astype(BF16)

    first_matmul(0)
    for p in range(n_pair):
        if p + 1 < n_pair:
            first_matmul(p + 1)
        gate_rows(p, 0)
        gate_rows(p, 1)
        if p % 2 == 1:
            lo, hi = (p - 1) * PAIR, (p + 1) * PAIR
            acc_ref[...] += _dot(pvt_ref[:, lo:hi], wt_ref[lo:hi, :])

    @pl.when(c == pl.num_programs(1) - 1)
    def _():
        mod = mod_ref[0]
        x2 = x1_ref[...] + mod[:, 5 * D:6 * D] * acc_ref[...].T
        y_ref[...] = _rms(x2, gfin_ref[...])


def _peer_dense(h2t, sel, pu, pvt, x1, mod, g_final, tm, tiles_per_row):
    t, D = x1.shape
    mrows = mod.shape[1]
    ca = PEER_CA
    nch = ca * PEER_NKEYS
    e2, te, cc = sel
    mod_spec = (pl.BlockSpec((1, 1, 6 * D), lambda i, c: (i // tiles_per_row, 0, 0)) if mrows == 1
                else pl.BlockSpec((1, tm, 6 * D), lambda i, c: (0, i, 0)))
    row_spec = pl.BlockSpec((PEER_HEADS, PEER_NKEYS // ca, ca, tm), lambda i, c: (0, 0, 0, i))
    return pl.pallas_call(
        functools.partial(_peer_kernel, ca=ca),
        grid=(t // tm, PEER_N // nch),
        in_specs=[pl.BlockSpec((D, tm), lambda i, c: (0, i)),
                  pl.BlockSpec((PEER_HEADS, PEER_NKEYS, tm), lambda i, c: (0, 0, i)),
                  row_spec, row_spec,
                  pl.BlockSpec((nch, D), lambda i, c: (c, 0)),
                  pl.BlockSpec((D, nch), lambda i, c: (0, c)),
                  pl.BlockSpec((tm, D), lambda i, c: (i, 0)),
                  mod_spec,
                  pl.BlockSpec((1, D), lambda i, c: (0, 0))],
        out_specs=pl.BlockSpec((tm, D), lambda i, c: (i, 0)),
        out_shape=jax.ShapeDtypeStruct((t, D), F32),
        scratch_shapes=[pltpu.VMEM((D, tm), F32), pltpu.VMEM((2, PAIR, tm), F32), pltpu.VMEM((nch, tm), BF16)],
        compiler_params=_params("parallel", "arbitrary"),
        name="peer_dense",
    )(h2t, e2, te, cc, pu, pvt, x1, mod, g_final.reshape(1, D))


def _rot_half(w):
    half = MLA_ROPE // 2
    return jnp.concatenate([-w[..., half:], w[..., :half]], axis=-1)


def _prep_layer(w_in, w_uq, w_uk, w_uv):
    D = D_MODEL
    o_kr = Q_LORA + KV_LORA
    w_kr = w_in[:, o_kr:o_kr + MLA_ROPE]
    zpad = jnp.zeros((D, LANES - MLA_ROPE), F32)
    win = jnp.concatenate([w_in[:, :o_kr], w_in[:, o_kr + MLA_ROPE:], w_kr, zpad, _rot_half(w_kr), zpad],
                          axis=1).astype(BF16)
    H = MLA_HEADS
    r = w_uq.reshape(Q_LORA, H, MLA_NOPE + MLA_ROPE)
    nope, rope = r[..., :MLA_NOPE], r[..., MLA_NOPE:]
    z32 = jnp.zeros((Q_LORA, H, LANES - MLA_NOPE - MLA_ROPE), F32)
    w_q = jnp.concatenate([nope, rope, z32], axis=-1).reshape(Q_LORA, H * LANES)
    w_qr = jnp.concatenate([jnp.zeros_like(nope), _rot_half(rope), z32], axis=-1).reshape(Q_LORA, H * LANES)
    wq = jnp.concatenate([w_q, w_qr], axis=1).astype(BF16)
    uk = w_uk.reshape(KV_LORA, H, MLA_NOPE)
    w_k = jnp.concatenate([uk, jnp.zeros_like(uk)], axis=-1).reshape(KV_LORA, H * LANES)
    uv = w_uv.reshape(KV_LORA, H // 2, 2, MLA_V)
    zv = jnp.zeros((KV_LORA, H // 2, MLA_V), F32)
    v_lo = jnp.concatenate([uv[:, :, 0], zv], axis=-1).reshape(KV_LORA, H // 2 * LANES)
    v_hi = jnp.concatenate([zv, uv[:, :, 1]], axis=-1).reshape(KV_LORA, H // 2 * LANES)
    wkv = jnp.concatenate([w_k, v_lo, v_hi], axis=1).astype(BF16)
    eye_h = jnp.eye(H, dtype=F32)
    uk_t = jnp.transpose(uk, (1, 2, 0))
    top = jnp.concatenate([uk_t, jnp.zeros((H, MLA_NOPE, LANES), F32)], axis=-1)
    mid = jnp.concatenate([jnp.zeros((H, MLA_ROPE, KV_LORA), F32),
                           jnp.broadcast_to(jnp.eye(MLA_ROPE, LANES, dtype=F32), (H, MLA_ROPE, LANES))], axis=-1)
    bot = jnp.zeros((H, LANES - MLA_NOPE - MLA_ROPE, KV_LORA + LANES), F32)
    m_h = jnp.concatenate([top, mid, bot], axis=1)
    w_abs = jnp.einsum('hrc,hg->hrgc', m_h, eye_h).reshape(H * LANES, H * (KV_LORA + LANES))
    uv_h = jnp.transpose(w_uv.reshape(KV_LORA, H, MLA_V), (1, 0, 2))
    w_uvbd = jnp.einsum('hcv,hg->hcgv', uv_h, eye_h).reshape(H * KV_LORA, H * MLA_V)
    return win, wq, wkv, w_abs, w_uvbd


def _rope_table(pos):
    inv = ROPE_THETA ** (-jnp.arange(0, MLA_ROPE, 2, dtype=F32) / MLA_ROPE)
    ang = pos.astype(F32)[:, None] * inv[None, :]
    cos, sin = jnp.cos(ang), jnp.sin(ang)
    n = pos.shape[0]
    z = lambda w: jnp.zeros((n, w), F32)
    c_q = MLA_SCALE * jnp.concatenate([jnp.ones((n, MLA_NOPE), F32), cos, cos, z(32)], axis=1)
    s_q = MLA_SCALE * jnp.concatenate([z(MLA_NOPE), sin, sin, z(32)], axis=1)
    c_k = jnp.concatenate([cos, cos, z(96)], axis=1)
    s_k = jnp.concatenate([sin, sin, z(96)], axis=1)
    return jnp.concatenate([c_q, s_q, c_k, s_k], axis=1)


def _pick(n, prefs):
    for p in prefs:
        if n % p == 0:
            return p
    return n


def kernel(x_prompt, x_sample, c_prompt, c_sample, cache_ckv, cache_kr, cache_dk, cache_dv, page_table, w_ada, b_ada, norm_attn_g, norm_ffn_g, w_in, mla_qnorm_g, w_uq, mla_kvnorm_g, w_uk, w_uv, w_o_mla, diff_lq1, diff_lk1, diff_lq2, diff_lk2, diff_subln_g, w_o_diff, w_out, w_pq, peer_keys, peer_u, peer_v, norm_final_g):
    D = D_MODEL
    nb, s_len, _ = x_prompt.shape
    n_seq, dec_seq, _ = x_sample.shape
    assert dec_seq == 1
    depth = w_ada.shape[0]
    n_pages = page_table.shape[1]
    past_len = n_pages * PAGE_SIZE
    t_p = nb * s_len

    tab_p = _rope_table(jnp.arange(s_len))
    tab_s = _rope_table(jnp.full((n_seq,), past_len, jnp.int32))
    tm_in = _pick(s_len, (256, 128))
    tq = _pick(s_len, (256, 128))
    tm_post = _pick(s_len, (256, 128))
    te = _pick(s_len, (256, 128))
    tm_peer = _pick(s_len, (512, 256, 128))
    pps = _pick(n_pages, (16, 8, 4, 2, 1))

    xp = x_prompt
    xs = x_sample.reshape(1, n_seq, D)
    c_all = jnp.concatenate([c_prompt, c_sample], axis=0)
    new_p, new_s = ([], [], [], []), ([], [], [], [])
    yp = ys = None
    for l in range(depth):
        lam_init = 0.8 - 0.6 * math.exp(-0.3 * l)
        lam = (jnp.exp(jnp.sum(diff_lq1[l] * diff_lk1[l])) - jnp.exp(jnp.sum(diff_lq2[l] * diff_lk2[l]))
               + lam_init).astype(F32).reshape(1)
        win, wq, wkv, w_abs, w_uvbd = _prep_layer(w_in[l], w_uq[l], w_uk[l], w_uv[l])
        wom, wod, wout = w_o_mla[l].astype(BF16), w_o_diff[l].astype(BF16), w_out[l].astype(BF16)
        wpq = w_pq[l].astype(BF16)
        keys = peer_keys[l].reshape(2 * PEER_HEADS, PEER_NKEYS, PEER_DKEY // 2).astype(BF16)
        pu = peer_u[l].astype(BF16)
        pvt = peer_v[l].T.astype(BF16)
        last = l == depth - 1
        g_fin = norm_final_g if last else jnp.ones((D,), F32)

        mod = _small_mm(c_all, w_ada[l], b_ada[l], pre_silu=True, tn=1024)
        mod_p = mod[:nb].reshape(nb, 1, 6 * D)
        mod_s = mod[nb:].reshape(1, n_seq, 6 * D)

        def mixer(x, modx, tab, tm):
            return _inproj(x, modx, tab, norm_attn_g[l], win, mla_qnorm_g[l], wq, mla_kvnorm_g[l], wkv, tm)

        def ffn(om, od, gates, x2d, modx, tm_a, tm_b, te_, tiles_per_row):
            x1, h2t, st = _post(om, od, gates, x2d, modx, wom, wod, wout, norm_ffn_g[l], wpq, keys, tm_a,
                                max(tiles_per_row * tm_b // tm_a, 1))
            sel = _peer_select(st, te_)
            return _peer_dense(h2t, sel, pu, pvt, x1, modx, g_fin, tm_b, tiles_per_row)

        (ckv, kr, dk, dv, q, k, vlo, vhi, dq, dkb, dvb, gates) = mixer(xp, mod_p, tab_p, tm_in)
        om, od = _prompt_attention(lam, q, k, vlo, vhi, dq, dkb, dvb, diff_subln_g[l], lam_init, tq)
        yp = ffn(om.reshape(t_p, 512), od.reshape(t_p, 1024), gates.reshape(t_p, 2048), xp.reshape(t_p, D),
                 mod_p, tm_post, tm_peer, te, s_len // tm_peer)
        for lst, r in zip(new_p, (ckv, kr, dk.reshape(nb, s_len, DIFF_KV_HEADS, 2 * DIFF_QK),
                                  dv.reshape(nb, s_len, DIFF_KV_HEADS, DIFF_V))):
            lst.append(r)

        (ckv_s, kr_s, dk_s, dv_s, q_s, _, _, _, dq_s, _, _, gates_s) = mixer(xs, mod_s, tab_s, n_seq)
        qlat = _small_mm(q_s[0].astype(F32), w_abs, tn=1024).reshape(n_seq, MLA_HEADS, KV_LORA + LANES)
        dq4 = dq_s[0].astype(F32).reshape(n_seq, DIFF_KV_HEADS, DIFF_GROUP, 2, DIFF_QK)
        zq = jnp.zeros_like(dq4[:, :, :, 0])
        qd = jnp.concatenate([jnp.concatenate([dq4[:, :, :, 0], zq], axis=-1),
                              jnp.concatenate([zq, dq4[:, :, :, 1]], axis=-1)],
                             axis=2).reshape(n_seq, DIFF_KV_HEADS * MLA_HEADS, LANES)
        olat, od_s = _decode_attention(
            page_table, lam, qlat, qd, ckv_s.reshape(n_seq, 1, KV_LORA), kr_s.reshape(n_seq, 1, MLA_ROPE),
            dk_s.reshape(n_seq, 1, 256), dv_s.reshape(n_seq, 1, 256), diff_subln_g[l],
            cache_ckv, cache_kr, cache_dk, cache_dv, l, lam_init, pps)
        om_s = _small_mm(olat.reshape(n_seq, MLA_HEADS * KV_LORA), w_uvbd).astype(BF16)
        ys = ffn(om_s, od_s.reshape(n_seq, 1024).astype(BF16), gates_s[0], xs[0], mod_s, n_seq, n_seq, n_seq, 1)
        for lst, r in zip(new_s, (ckv_s.reshape(n_seq, 1, KV_LORA), kr_s.reshape(n_seq, 1, MLA_ROPE),
                                  dk_s.reshape(n_seq, 1, DIFF_KV_HEADS, 2 * DIFF_QK),
                                  dv_s.reshape(n_seq, 1, DIFF_KV_HEADS, DIFF_V))):
            lst.append(r)
        if not last:
            raise NotImplementedError("multi-layer stacking needs the un-normalised residual stream")

    y_prompt = yp.reshape(nb, s_len, D)
    y_sample = ys.reshape(n_seq, 1, D)
    stack = lambda t: jnp.stack(t, axis=0)
    return (y_prompt, y_sample, *[stack(t) for t in new_p], *[stack(t) for t in new_s])
```

```python
import functools
import math

import jax
import jax.numpy as jnp
from jax import lax
from jax.experimental import pallas as pl
from jax.experimental.pallas import tpu as pltpu

D_MODEL = 1024
PAGE_SIZE = 128
MLA_HEADS = 8
MLA_NOPE = 64
MLA_ROPE = 32
MLA_V = 64
Q_LORA = 384
KV_LORA = 256
ROPE_THETA = 10000.0
MLA_SCALE = (MLA_NOPE + MLA_ROPE) ** -0.5
DIFF_HEADS = 8
DIFF_KV_HEADS = 2
DIFF_GROUP = DIFF_HEADS // DIFF_KV_HEADS
DIFF_QK = 64
DIFF_V = 2 * DIFF_QK
PEER_HEADS = 8
PEER_NKEYS = 128
PEER_N = PEER_NKEYS * PEER_NKEYS
PEER_DKEY = 256
PEER_TOPK = 16
RMS_EPS = 1e-6

LANES = 128
VMEM_LIMIT = 56 * 1024 * 1024
IN_PAD = 4480
NEG_INF = float("-inf")

BF16 = jnp.bfloat16
F32 = jnp.float32


def _params(*sem):
    return pltpu.CompilerParams(dimension_semantics=sem, vmem_limit_bytes=VMEM_LIMIT)


def _dot(a, b):
    return jnp.dot(a, b, preferred_element_type=F32)


def _dot_nt(a, b):
    return lax.dot_general(a, b, (((1,), (1,)), ((), ())), preferred_element_type=F32)


def _rms(x, g):
    return x * lax.rsqrt(jnp.mean(x * x, axis=-1, keepdims=True) + RMS_EPS) * g


def _small_mm_kernel(a_ref, w_ref, b_ref, o_ref, *, pre_silu):
    a = a_ref[...]
    if pre_silu:
        a = a * jax.nn.sigmoid(a)
    o_ref[...] = _dot(a.astype(BF16), w_ref[...].astype(BF16)) + b_ref[...]


def _small_mm(a, w, bias=None, pre_silu=False, tn=512):
    m, k = a.shape
    n = w.shape[1]
    tn = min(tn, n)
    if bias is None:
        bias = jnp.zeros((n,), F32)
    return pl.pallas_call(
        functools.partial(_small_mm_kernel, pre_silu=pre_silu),
        grid=(n // tn,),
        in_specs=[pl.BlockSpec((m, k), lambda j: (0, 0)),
                  pl.BlockSpec((k, tn), lambda j: (0, j)),
                  pl.BlockSpec((1, tn), lambda j: (0, j))],
        out_specs=pl.BlockSpec((m, tn), lambda j: (0, j)),
        out_shape=jax.ShapeDtypeStruct((m, n), F32),
        compiler_params=_params("arbitrary"),
        name="small_mm",
    )(a, w, bias.reshape(1, n))


C_CQ, C_CKV, C_DQ, C_DK, C_DV, C_GA, C_GB, C_KR, C_KRR = 0, 384, 640, 1664, 1920, 2176, 3200, 4224, 4352


def _inproj_kernel(x_ref, mod_ref, tab_ref, gattn_ref, win_ref, qng_ref, wq_ref, kvg_ref, wkv_ref,
                   ckv_ref, kr_ref, dk_ref, dv_ref, q_ref, k_ref, vlo_ref, vhi_ref, dq_ref,
                   dkb_ref, dvb_ref, gates_ref):
    D = D_MODEL
    x = x_ref[0]
    mod = mod_ref[0]
    h = _rms(x, gattn_ref[...]) * (1.0 + mod[:, D:2 * D]) + mod[:, 0:D]
    hb = h.astype(BF16)
    tab = tab_ref[...]
    c_q, s_q, c_k, s_k = (tab[:, i * LANES:(i + 1) * LANES] for i in range(4))

    def proj(lo, hi):
        return _dot(hb, win_ref[:, lo:hi])

    gates_ref[0] = jax.nn.sigmoid(proj(C_GA, C_KR))
    dq_ref[0] = (proj(C_DQ, C_DK) * (DIFF_QK ** -0.5)).astype(BF16)
    dk = proj(C_DK, C_DV)
    dv = proj(C_DV, C_GA)
    dk_ref[0] = dk
    dv_ref[0] = dv
    dkb_ref[0] = dk.astype(BF16)
    dvb_ref[0] = dv.astype(BF16)

    kr2 = proj(C_KR, IN_PAD)
    kr = kr2[:, :LANES] * c_k + kr2[:, LANES:] * s_k
    kr_ref[0] = kr[:, :MLA_ROPE]
    kr_sh = pltpu.roll(kr, MLA_NOPE, 1)

    ckv = _rms(proj(C_CKV, C_DQ), kvg_ref[...])
    ckv_ref[0] = ckv
    kv = _dot(ckv.astype(BF16), wkv_ref[...])
    for hd in range(MLA_HEADS):
        sl = slice(hd * LANES, (hd + 1) * LANES)
        k_ref[0, :, sl] = (kv[:, sl] + kr_sh).astype(BF16)
    vlo_ref[0] = kv[:, 1024:1536].astype(BF16)
    vhi_ref[0] = kv[:, 1536:2048].astype(BF16)

    cq = _rms(proj(C_CQ, C_CKV), qng_ref[...]).astype(BF16)
    q2 = _dot(cq, wq_ref[...])
    for hd in range(MLA_HEADS):
        sl = slice(hd * LANES, (hd + 1) * LANES)
        sr = slice(1024 + hd * LANES, 1024 + (hd + 1) * LANES)
        q_ref[0, :, sl] = (q2[:, sl] * c_q + q2[:, sr] * s_q).astype(BF16)


def _inproj(x, mod, tab, g_attn, win, qn_g, wq, kvn_g, wkv, tm):
    nb, s, D = x.shape
    mrows = mod.shape[1]
    grid = (nb, s // tm)
    tok = lambda w, dt: jax.ShapeDtypeStruct((nb, s, w), dt)
    tspec = lambda w: pl.BlockSpec((1, tm, w), lambda b, i: (b, i, 0))
    const = lambda shape: pl.BlockSpec(shape, lambda b, i: (0,) * len(shape))
    mod_spec = (pl.BlockSpec((1, 1, 6 * D), lambda b, i: (b, 0, 0)) if mrows == 1
                else pl.BlockSpec((1, tm, 6 * D), lambda b, i: (b, i, 0)))
    widths = [(KV_LORA, F32), (MLA_ROPE, F32), (256, F32), (256, F32), (1024, BF16), (1024, BF16),
              (512, BF16), (512, BF16), (1024, BF16), (256, BF16), (256, BF16), (2048, F32)]
    return pl.pallas_call(
        _inproj_kernel,
        grid=grid,
        in_specs=[tspec(D), mod_spec, pl.BlockSpec((tm, 4 * LANES), lambda b, i: (i, 0)),
                  const((1, D)), const((D, IN_PAD)), const((1, Q_LORA)), const((Q_LORA, 2048)),
                  const((1, KV_LORA)), const((KV_LORA, 2048))],
        out_specs=[tspec(w) for w, _ in widths],
        out_shape=[tok(w, dt) for w, dt in widths],
        compiler_params=_params("parallel", "arbitrary"),
        name="inproj",
    )(x, mod, tab, g_attn.reshape(1, D), win, qn_g.reshape(1, Q_LORA), wq, kvn_g.reshape(1, KV_LORA), wkv)


N_CHAIN = MLA_HEADS + 2 * DIFF_HEADS
CHAIN_GROUPS = ((tuple(range(MLA_HEADS)), tuple(range(DIFF_HEADS))),)


def _softmax_update(s, v, m_ref, l_ref, acc_ref):
    reps = s.shape[1] // LANES
    m_prev = m_ref[...]
    m_next = jnp.maximum(m_prev, jnp.max(s, axis=1, keepdims=True))
    p = jnp.exp(s - jnp.tile(m_next, (1, reps)))
    alpha = jnp.exp(m_prev - m_next)
    l_ref[...] = alpha * l_ref[...] + jnp.sum(p, axis=1, keepdims=True)
    m_ref[...] = m_next
    acc_ref[...] = alpha * acc_ref[...] + _dot(p.astype(BF16), v)


def _attn_kernel(lam_ref, q_ref, k_ref, vlo_ref, vhi_ref, dq_ref, dk_ref, dv_ref, subg_ref,
                 om_ref, od_ref, qd_ref, m_ref, l_ref, acc_ref, *, tq, tk, lam_init):
    q0 = pl.program_id(1) * tq
    n_full = q0 // tk
    lam = lam_ref[0]

    m_ref[...] = jnp.full(m_ref.shape, NEG_INF, F32)
    l_ref[...] = jnp.zeros(l_ref.shape, F32)
    acc_ref[...] = jnp.zeros(acc_ref.shape, F32)
    lane = lax.broadcasted_iota(jnp.int32, (tq, LANES), 1)
    for hd in range(DIFF_HEADS):
        qd = dq_ref[0, :, hd * LANES:(hd + 1) * LANES]
        zero = jnp.zeros_like(qd)
        qd_ref[2 * hd] = jnp.where(lane < DIFF_QK, qd, zero)
        qd_ref[2 * hd + 1] = jnp.where(lane >= DIFF_QK, qd, zero)

    col = lax.broadcasted_iota(jnp.int32, (tq, tk), 1)
    row = lax.broadcasted_iota(jnp.int32, (tq, tk), 0)

    def chunk(j, diagonal, mla_heads, diff_heads):
        off = pl.multiple_of(j * tk, tk)
        rows = lambda ref, lane0: ref[0, pl.ds(off, tk), lane0:lane0 + LANES]
        rel = col - row + (off - q0)
        for hd in mla_heads:
            s = _dot_nt(q_ref[0, :, hd * LANES:(hd + 1) * LANES], rows(k_ref, hd * LANES))
            if diagonal:
                s = jnp.where(rel <= 0, s, NEG_INF)
            v = rows(vlo_ref if hd % 2 == 0 else vhi_ref, (hd // 2) * LANES)
            _softmax_update(s, v, m_ref.at[hd], l_ref.at[hd], acc_ref.at[hd])
        dist = rel.astype(F32)
        if diagonal:
            dist = jnp.where(rel <= 0, dist, NEG_INF)
        for hd in diff_heads:
            kv = hd // DIFF_GROUP
            bias = dist * (2.0 ** (-8.0 * (hd + 1) / DIFF_HEADS))
            kk = rows(dk_ref, kv * LANES)
            vv = rows(dv_ref, kv * LANES)
            for mp in range(2):
                c = MLA_HEADS + 2 * hd + mp
                s = _dot_nt(qd_ref[2 * hd + mp], kk) + bias
                _softmax_update(s, vv, m_ref.at[c], l_ref.at[c], acc_ref.at[c])

    for mla_heads, diff_heads in CHAIN_GROUPS:
        def body(j, carry, mla_heads=mla_heads, diff_heads=diff_heads):
            chunk(j, False, mla_heads, diff_heads)
            return carry

        lax.fori_loop(0, n_full, body, 0)
        chunk(n_full, True, mla_heads, diff_heads)

    inv = lambda c: acc_ref[c] / l_ref[c]
    for pair in range(MLA_HEADS // 2):
        om_ref[0, :, pair * LANES:(pair + 1) * LANES] = (inv(2 * pair) + inv(2 * pair + 1)).astype(BF16)
    for hd in range(DIFF_HEADS):
        c = MLA_HEADS + 2 * hd
        o = inv(c) - lam * inv(c + 1)
        od_ref[0, :, hd * LANES:(hd + 1) * LANES] = (_rms(o, subg_ref[...]) * (1.0 - lam_init)).astype(BF16)


def _prompt_attention(lam, q, k, vlo, vhi, dq, dkb, dvb, subln_g, lam_init, tq):
    nb, s, _ = q.shape
    qspec = lambda w: pl.BlockSpec((1, tq, w), lambda b, i: (b, i, 0))
    full = lambda w: pl.BlockSpec((1, s, w), lambda b, i: (b, 0, 0))
    state = pltpu.VMEM((N_CHAIN, tq, LANES), F32)
    return pl.pallas_call(
        functools.partial(_attn_kernel, tq=tq, tk=tq, lam_init=lam_init),
        grid=(nb, s // tq),
        in_specs=[pl.BlockSpec(memory_space=pltpu.SMEM), qspec(1024), full(1024), full(512), full(512),
                  qspec(1024), full(256), full(256), pl.BlockSpec((1, DIFF_V), lambda b, i: (0, 0))],
        out_specs=[qspec(512), qspec(1024)],
        out_shape=[jax.ShapeDtypeStruct((nb, s, 512), BF16), jax.ShapeDtypeStruct((nb, s, 1024), BF16)],
        scratch_shapes=[pltpu.VMEM((2 * DIFF_HEADS, tq, LANES), BF16), state, state, state],
        compiler_params=_params("parallel", "arbitrary"),
        name="prompt_attn",
    )(lam, q, k, vlo, vhi, dq, dkb, dvb, subln_g.reshape(1, DIFF_V))


N_CACHE = 4


def _decode_kernel(pt_ref, lam_ref, qlat_ref, qd_ref, nckv_ref, nkr_ref, ndk_ref, ndv_ref, subg_ref,
                   ckv_hbm, krt_hbm, dk_hbm, dv_hbm, olat_ref, od_ref,
                   m_m, l_m, a_m, m_d, l_d, a_d, ckv_buf, krt_buf, dk_buf, dv_buf, sem,
                   *, pps, past_len, lam_init, layer):
    seq = pl.program_id(0)
    step = pl.program_id(1)
    n_steps = pl.num_programs(1)
    R = MLA_HEADS
    caches = (ckv_hbm, krt_hbm, dk_hbm, dv_hbm)
    bufs = (ckv_buf, krt_buf, dk_buf, dv_buf)

    def page_copies(q_seq, q_step, q_slot):
        copies = []
        for j in range(pps):
            page = pt_ref[q_seq, q_step * pps + j]
            for k in range(N_CACHE):
                copies.append(pltpu.make_async_copy(caches[k].at[layer, page], bufs[k].at[q_slot, j],
                                                    sem.at[q_slot, k]))
        return copies

    g = seq * n_steps + step
    slot = g % 2

    @pl.when(g == 0)
    def _():
        for cp in page_copies(0, 0, 0):
            cp.start()

    @pl.when(g + 1 < pl.num_programs(0) * n_steps)
    def _():
        nxt = g + 1
        for cp in page_copies(nxt // n_steps, nxt % n_steps, 1 - slot):
            cp.start()

    for cp in page_copies(seq, step, slot):
        cp.wait()
    pages = [bufs[k].at[slot, j] for j in range(pps) for k in range(N_CACHE)]
    RD = DIFF_KV_HEADS * R
    W2 = DIFF_KV_HEADS * PAGE_SIZE

    qlat = qlat_ref[0]
    qd = qd_ref[0]
    kv_row = lax.broadcasted_iota(jnp.int32, (RD, LANES), 0) // R

    @pl.when(step == 0)
    def _():
        s = (jnp.sum(qlat[:, :KV_LORA] * nckv_ref[0], axis=-1, keepdims=True)
             + jnp.sum(qlat[:, KV_LORA:KV_LORA + MLA_ROPE] * nkr_ref[0], axis=-1, keepdims=True))
        m_m[...] = jnp.broadcast_to(s, (R, LANES))
        l_m[...] = jnp.ones((R, LANES), F32)
        a_m[...] = jnp.broadcast_to(nckv_ref[0], (R, KV_LORA))
        ndk, ndv = ndk_ref[0], ndv_ref[0]
        own = lambda t: jnp.where(kv_row == 0, jnp.broadcast_to(t[:, :LANES], (RD, LANES)),
                                  jnp.broadcast_to(t[:, LANES:], (RD, LANES)))
        m_d[...] = jnp.broadcast_to(jnp.sum(qd * own(ndk), axis=-1, keepdims=True), (RD, LANES))
        l_d[...] = jnp.ones((RD, LANES), F32)
        a_d[...] = own(ndv)

    def update(s_list, v_list, m_ref, l_ref, a_ref):
        m_old = m_ref[...]
        mx = s_list[0]
        for s in s_list[1:]:
            mx = jnp.maximum(mx, s)
        m_new = jnp.maximum(m_old, jnp.max(mx, axis=-1, keepdims=True))
        alpha = jnp.exp(m_old - m_new)
        reps = s_list[0].shape[1] // LANES
        m_b = jnp.tile(m_new, (1, reps))
        psum = None
        pv = None
        for s, v in zip(s_list, v_list):
            p = jnp.exp(s - m_b)
            psum = p if psum is None else psum + p
            d = _dot(p.astype(BF16), v)
            pv = d if pv is None else pv + d
        m_ref[...] = m_new
        l_ref[...] = alpha * l_ref[...] + jnp.sum(psum, axis=-1, keepdims=True)
        a_ref[...] = jnp.tile(alpha, (1, a_ref.shape[1] // LANES)) * a_ref[...] + pv

    qc = qlat[:, :KV_LORA].astype(BF16)
    qr = qlat[:, KV_LORA:KV_LORA + MLA_ROPE].astype(BF16)
    ckvs = [pages[4 * j][...].astype(BF16) for j in range(pps)]
    s_m = [_dot_nt(qc, ckvs[j]) + _dot(qr, pages[4 * j + 1][...].astype(BF16)) for j in range(pps)]
    update(s_m, ckvs, m_m, l_m, a_m)

    col = lax.broadcasted_iota(jnp.int32, (RD, W2), 1)
    row = lax.broadcasted_iota(jnp.int32, (RD, W2), 0)
    hd = (row // R) * DIFF_GROUP + row % DIFF_GROUP
    slope = jnp.exp2(-8.0 * (hd + 1).astype(F32) / DIFF_HEADS)
    base = jnp.where(col % DIFF_KV_HEADS == row // R, slope * (col // DIFF_KV_HEADS).astype(F32), NEG_INF)
    qdb = qd.astype(BF16)
    s_d, v_d = [], []
    for j in range(pps):
        dist0 = ((step * pps + j) * PAGE_SIZE - past_len).astype(F32)
        s_d.append(_dot_nt(qdb, pages[4 * j + 2][...].astype(BF16)) + (base + slope * dist0))
        v_d.append(pages[4 * j + 3][...].astype(BF16))
    update(s_d, v_d, m_d, l_d, a_d)

    @pl.when(step == n_steps - 1)
    def _():
        olat_ref[0] = a_m[...] / jnp.tile(l_m[...], (1, KV_LORA // LANES))
        lam = lam_ref[0]
        o2 = a_d[...] / l_d[...]
        for kv in range(DIFF_KV_HEADS):
            o = o2[kv * R:kv * R + DIFF_GROUP] - lam * o2[kv * R + DIFF_GROUP:(kv + 1) * R]
            od_ref[0, kv] = _rms(o, subg_ref[...]) * (1.0 - lam_init)


def _decode_attention(page_table, lam, qlat, qd, nckv, nkr, ndk, ndv, subln_g, cache_ckv, cache_kr,
                      cache_dk, cache_dv, layer, lam_init, pps):
    n_seq, n_pages = page_table.shape
    past_len = n_pages * PAGE_SIZE
    R = MLA_HEADS
    RD = DIFF_KV_HEADS * R
    W2 = DIFF_KV_HEADS * PAGE_SIZE
    n_layers, n_pool = cache_ckv.shape[:2]
    cache_krt = jnp.swapaxes(cache_kr, 2, 3)
    cache_dk2 = cache_dk.reshape(n_layers, n_pool, W2, 2 * DIFF_QK)
    cache_dv2 = cache_dv.reshape(n_layers, n_pool, W2, DIFF_V)
    seq3 = lambda w: pl.BlockSpec((1, 1, w), lambda n, s, pt: (n, 0, 0))
    in_specs = [pl.BlockSpec(memory_space=pltpu.SMEM),
                pl.BlockSpec((1, R, 384), lambda n, s, pt: (n, 0, 0)),
                pl.BlockSpec((1, RD, LANES), lambda n, s, pt: (n, 0, 0)),
                seq3(KV_LORA), seq3(MLA_ROPE), seq3(256), seq3(256),
                pl.BlockSpec((1, DIFF_V), lambda n, s, pt: (0, 0))]
    args = [lam, qlat, qd, nckv, nkr, ndk, ndv, subln_g.reshape(1, DIFF_V),
            cache_ckv, cache_krt, cache_dk2, cache_dv2]
    in_specs += [pl.BlockSpec(memory_space=pl.ANY)] * N_CACHE
    page_buf = lambda rows, width: pltpu.VMEM((2, pps, rows, width), F32)
    grid_spec = pltpu.PrefetchScalarGridSpec(
        num_scalar_prefetch=1,
        grid=(n_seq, n_pages // pps),
        in_specs=in_specs,
        out_specs=[pl.BlockSpec((1, R, KV_LORA), lambda n, s, pt: (n, 0, 0)),
                   pl.BlockSpec((1, DIFF_KV_HEADS, DIFF_GROUP, LANES), lambda n, s, pt: (n, 0, 0, 0))],
        scratch_shapes=[pltpu.VMEM((R, LANES), F32), pltpu.VMEM((R, LANES), F32), pltpu.VMEM((R, KV_LORA), F32),
                        pltpu.VMEM((RD, LANES), F32), pltpu.VMEM((RD, LANES), F32), pltpu.VMEM((RD, LANES), F32),
                        page_buf(PAGE_SIZE, KV_LORA), page_buf(MLA_ROPE, PAGE_SIZE),
                        page_buf(W2, 2 * DIFF_QK), page_buf(W2, DIFF_V),
                        pltpu.SemaphoreType.DMA((2, N_CACHE))],
    )
    return pl.pallas_call(
        functools.partial(_decode_kernel, pps=pps, past_len=past_len, lam_init=lam_init, layer=layer),
        grid_spec=grid_spec,
        out_shape=[jax.ShapeDtypeStruct((n_seq, R, KV_LORA), F32),
                   jax.ShapeDtypeStruct((n_seq, DIFF_KV_HEADS, DIFF_GROUP, LANES), F32)],
        compiler_params=_params("arbitrary", "arbitrary"),
        name="decode_attn",
    )(page_table, *args)


def _post_kernel(om_ref, od_ref, gates_ref, x_ref, mod_ref, wom_ref, wod_ref, wout_ref, gffn_ref, wpq_ref,
                 keys_ref, x1_ref, h2t_ref, st_ref):
    D = D_MODEL
    mod = mod_ref[0]
    gates = gates_ref[...]
    merged = gates[:, :D] * _dot(om_ref[...], wom_ref[...]) + gates[:, D:] * _dot(od_ref[...], wod_ref[...])
    x1 = x_ref[...] + mod[:, 2 * D:3 * D] * _dot(merged.astype(BF16), wout_ref[...])
    x1_ref[...] = x1
    h2 = _rms(x1, gffn_ref[...]) * (1.0 + mod[:, 4 * D:5 * D]) + mod[:, 3 * D:4 * D]
    h2t_ref[...] = h2.T.astype(BF16)
    pq = _dot(h2.astype(BF16), wpq_ref[...]).astype(BF16)
    for i in range(2 * PEER_HEADS):
        st_ref[i] = _dot_nt(keys_ref[i], pq[:, i * LANES:(i + 1) * LANES])


def _post(om, od, gates, x, mod, wom, wod, wout, g_ffn, wpq, keys, tm, tiles_per_row):
    t, D = x.shape
    mrows = mod.shape[1]
    tok = lambda w: pl.BlockSpec((tm, w), lambda i: (i, 0))
    const = lambda shape: pl.BlockSpec(shape, lambda i: (0,) * len(shape))
    mod_spec = (pl.BlockSpec((1, 1, 6 * D), lambda i: (i // tiles_per_row, 0, 0)) if mrows == 1
                else pl.BlockSpec((1, tm, 6 * D), lambda i: (0, i, 0)))
    nk = 2 * PEER_HEADS
    return pl.pallas_call(
        _post_kernel,
        grid=(t // tm,),
        in_specs=[tok(512), tok(1024), tok(2048), tok(D), mod_spec, const((512, D)), const((1024, D)),
                  const((D, D)), const((1, D)), const((D, PEER_HEADS * PEER_DKEY)),
                  const((nk, PEER_NKEYS, PEER_DKEY // 2))],
        out_specs=[tok(D), pl.BlockSpec((D, tm), lambda i: (0, i)),
                   pl.BlockSpec((nk, PEER_NKEYS, tm), lambda i: (0, 0, i))],
        out_shape=[jax.ShapeDtypeStruct((t, D), F32), jax.ShapeDtypeStruct((D, t), BF16),
                   jax.ShapeDtypeStruct((nk, PEER_NKEYS, t), F32)],
        compiler_params=_params("parallel"),
        name="post_attn",
    )(om, od, gates, x, mod, wom, wod, wout, g_ffn.reshape(1, D), wpq, keys)


N_EXT = PEER_TOPK + 1
CAND = [(i, j) for i in range(N_EXT) for j in range(N_EXT) if (i + 1) * (j + 1) <= N_EXT]


def _extract_desc(s, n):
    out = []
    for _ in range(n):
        m = jnp.max(s, axis=0, keepdims=True)
        out.append(m)
        s = jnp.where(s == m, NEG_INF, s)
    return out


CAND_ROWS = -(-len(CAND) // 8) * 8
PEER_CA = 16


def _select_kernel(st_ref, e2_ref, te_ref, cc_ref, cand_ref):
    te = st_ref.shape[-1]
    cand_ref[...] = jnp.full(cand_ref.shape, NEG_INF, F32)

    def head(h, carry):
        s1 = st_ref[2 * h]
        s2 = st_ref[2 * h + 1]
        t1 = _extract_desc(s1, N_EXT)
        t2 = _extract_desc(s2, N_EXT)
        for r, (i, j) in enumerate(CAND):
            cand_ref[r:r + 1, :] = t1[i] + t2[j]
        top = _extract_desc(cand_ref[...], N_EXT)
        z = jnp.ones((1, te), F32)
        for c in top[1:PEER_TOPK]:
            z = z + jnp.exp(c - top[0])
        e2_ref[h] = jnp.exp(s2 - t2[0])
        thr_e = jnp.exp((0.5 * (top[PEER_TOPK - 1] + top[PEER_TOPK]) - s1) - t2[0])
        cc = jnp.exp(s1 - t1[0]) * (0.5 / z)
        for k in range(PEER_NKEYS // PEER_CA):
            te_ref[h, k] = thr_e[k * PEER_CA:(k + 1) * PEER_CA]
            cc_ref[h, k] = cc[k * PEER_CA:(k + 1) * PEER_CA]
        return carry

    lax.fori_loop(0, PEER_HEADS, head, 0)


def _peer_select(st, te):
    nk, n, t = st.shape
    nck = n // PEER_CA
    spec4 = pl.BlockSpec((PEER_HEADS, nck, PEER_CA, te), lambda i: (0, 0, 0, i))
    shape4 = jax.ShapeDtypeStruct((PEER_HEADS, nck, PEER_CA, t), F32)
    return pl.pallas_call(
        _select_kernel,
        grid=(t // te,),
        in_specs=[pl.BlockSpec((nk, n, te), lambda i: (0, 0, i))],
        out_specs=[pl.BlockSpec((PEER_HEADS, n, te), lambda i: (0, 0, i)), spec4, spec4],
        out_shape=[jax.ShapeDtypeStruct((PEER_HEADS, n, t), F32), shape4, shape4],
        scratch_shapes=[pltpu.VMEM((CAND_ROWS, te), F32)],
        compiler_params=_params("parallel"),
        name="peer_select",
    )(st)


SUB = 8
PAIR = 2 * PEER_NKEYS


def _peer_kernel(h2t_ref, e2_ref, te_ref, cc_ref, pu_ref, pvt_ref, x1_ref, mod_ref, gfin_ref, y_ref,
                 acc_ref, ut_ref, wt_ref, *, ca):
    D = D_MODEL
    c = pl.program_id(1)
    tm = h2t_ref.shape[1]
    n_pair = ca // 2

    @pl.when(c == 0)
    def _():
        acc_ref[...] = jnp.zeros_like(acc_ref)

    def first_matmul(p):
        ut_ref[p % 2] = _dot(pu_ref[p * PAIR:(p + 1) * PAIR, :], h2t_ref[...])

    def gate_rows(p, ah):
        al = 2 * p + ah
        row_of = lambda ref, h, lanes: jnp.broadcast_to(ref[h, c, al:al + 1, lanes], (SUB, LANES))
        for tg in range(tm // LANES):
            lanes = slice(tg * LANES, (tg + 1) * LANES)
            tes = [row_of(te_ref, h, lanes) for h in range(PEER_HEADS)]
            ccs = [row_of(cc_ref, h, lanes) for h in range(PEER_HEADS)]
            for bp in range(PEER_NKEYS // (2 * SUB)):
                ws = []
                for r0 in (2 * SUB * bp, 2 * SUB * bp + SUB):
                    g = None
                    for h in range(PEER_HEADS):
                        e2 = e2_ref[h, r0:r0 + SUB, lanes]
                        t = jnp.where(e2 > tes[h], e2 * ccs[h], 0.0)
                        g = t if g is None else g + t
                    u = ut_ref[p % 2, ah * PEER_NKEYS + r0:ah * PEER_NKEYS + r0 + SUB, lanes]
                    ws.append((g * u) * (1.0 + lax.erf(u * math.sqrt(0.5))))
                w0 = p * PAIR + ah * PEER_NKEYS + 2 * SUB * bp
                wt_ref[w0:w0 + 2 * SUB, lanes] = jnp.concatenate(ws, axis=0).astype(BF16)

    first_matmul(0)
    for p in range(n_pair):
        if p + 1 < n_pair:
            first_matmul(p + 1)
        gate_rows(p, 0)
        gate_rows(p, 1)
        if p % 2 == 1:
            lo, hi = (p - 1) * PAIR, (p + 1) * PAIR
            acc_ref[...] += _dot(pvt_ref[:, lo:hi], wt_ref[lo:hi, :])

    @pl.when(c == pl.num_programs(1) - 1)
    def _():
        mod = mod_ref[0]
        x2 = x1_ref[...] + mod[:, 5 * D:6 * D] * acc_ref[...].T
        y_ref[...] = _rms(x2, gfin_ref[...])


def _peer_dense(h2t, sel, pu, pvt, x1, mod, g_final, tm, tiles_per_row):
    t, D = x1.shape
    mrows = mod.shape[1]
    ca = PEER_CA
    nch = ca * PEER_NKEYS
    e2, te, cc = sel
    mod_spec = (pl.BlockSpec((1, 1, 6 * D), lambda i, c: (i // tiles_per_row, 0, 0)) if mrows == 1
                else pl.BlockSpec((1, tm, 6 * D), lambda i, c: (0, i, 0)))
    row_spec = pl.BlockSpec((PEER_HEADS, PEER_NKEYS // ca, ca, tm), lambda i, c: (0, 0, 0, i))
    return pl.pallas_call(
        functools.partial(_peer_kernel, ca=ca),
        grid=(t // tm, PEER_N // nch),
        in_specs=[pl.BlockSpec((D, tm), lambda i, c: (0, i)),
                  pl.BlockSpec((PEER_HEADS, PEER_NKEYS, tm), lambda i, c: (0, 0, i)),
                  row_spec, row_spec,
                  pl.BlockSpec((nch, D), lambda i, c: (c, 0)),
                  pl.BlockSpec((D, nch), lambda i, c: (0, c)),
                  pl.BlockSpec((tm, D), lambda i, c: (i, 0)),
                  mod_spec,
                  pl.BlockSpec((1, D), lambda i, c: (0, 0))],
        out_specs=pl.BlockSpec((tm, D), lambda i, c: (i, 0)),
        out_shape=jax.ShapeDtypeStruct((t, D), F32),
        scratch_shapes=[pltpu.VMEM((D, tm), F32), pltpu.VMEM((2, PAIR, tm), F32), pltpu.VMEM((nch, tm), BF16)],
        compiler_params=_params("parallel", "arbitrary"),
        name="peer_dense",
    )(h2t, e2, te, cc, pu, pvt, x1, mod, g_final.reshape(1, D))


def _rot_half(w):
    half = MLA_ROPE // 2
    return jnp.concatenate([-w[..., half:], w[..., :half]], axis=-1)


def _prep_layer(w_in, w_uq, w_uk, w_uv):
    D = D_MODEL
    o_kr = Q_LORA + KV_LORA
    w_kr = w_in[:, o_kr:o_kr + MLA_ROPE]
    zpad = jnp.zeros((D, LANES - MLA_ROPE), F32)
    win = jnp.concatenate([w_in[:, :o_kr], w_in[:, o_kr + MLA_ROPE:], w_kr, zpad, _rot_half(w_kr), zpad],
                          axis=1).astype(BF16)
    H = MLA_HEADS
    r = w_uq.reshape(Q_LORA, H, MLA_NOPE + MLA_ROPE)
    nope, rope = r[..., :MLA_NOPE], r[..., MLA_NOPE:]
    z32 = jnp.zeros((Q_LORA, H, LANES - MLA_NOPE - MLA_ROPE), F32)
    w_q = jnp.concatenate([nope, rope, z32], axis=-1).reshape(Q_LORA, H * LANES)
    w_qr = jnp.concatenate([jnp.zeros_like(nope), _rot_half(rope), z32], axis=-1).reshape(Q_LORA, H * LANES)
    wq = jnp.concatenate([w_q, w_qr], axis=1).astype(BF16)
    uk = w_uk.reshape(KV_LORA, H, MLA_NOPE)
    w_k = jnp.concatenate([uk, jnp.zeros_like(uk)], axis=-1).reshape(KV_LORA, H * LANES)
    uv = w_uv.reshape(KV_LORA, H // 2, 2, MLA_V)
    zv = jnp.zeros((KV_LORA, H // 2, MLA_V), F32)
    v_lo = jnp.concatenate([uv[:, :, 0], zv], axis=-1).reshape(KV_LORA, H // 2 * LANES)
    v_hi = jnp.concatenate([zv, uv[:, :, 1]], axis=-1).reshape(KV_LORA, H // 2 * LANES)
    wkv = jnp.concatenate([w_k, v_lo, v_hi], axis=1).astype(BF16)
    eye_h = jnp.eye(H, dtype=F32)
    uk_t = jnp.transpose(uk, (1, 2, 0))
    top = jnp.concatenate([uk_t, jnp.zeros((H, MLA_NOPE, LANES), F32)], axis=-1)
    mid = jnp.concatenate([jnp.zeros((H, MLA_ROPE, KV_LORA), F32),
                           jnp.broadcast_to(jnp.eye(MLA_ROPE, LANES, dtype=F32), (H, MLA_ROPE, LANES))], axis=-1)
    bot = jnp.zeros((H, LANES - MLA_NOPE - MLA_ROPE, KV_LORA + LANES), F32)
    m_h = jnp.concatenate([top, mid, bot], axis=1)
    w_abs = jnp.einsum('hrc,hg->hrgc', m_h, eye_h).reshape(H * LANES, H * (KV_LORA + LANES))
    uv_h = jnp.transpose(w_uv.reshape(KV_LORA, H, MLA_V), (1, 0, 2))
    w_uvbd = jnp.einsum('hcv,hg->hcgv', uv_h, eye_h).reshape(H * KV_LORA, H * MLA_V)
    return win, wq, wkv, w_abs, w_uvbd


def _rope_table(pos):
    inv = ROPE_THETA ** (-jnp.arange(0, MLA_ROPE, 2, dtype=F32) / MLA_ROPE)
    ang = pos.astype(F32)[:, None] * inv[None, :]
    cos, sin = jnp.cos(ang), jnp.sin(ang)
    n = pos.shape[0]
    z = lambda w: jnp.zeros((n, w), F32)
    c_q = MLA_SCALE * jnp.concatenate([jnp.ones((n, MLA_NOPE), F32), cos, cos, z(32)], axis=1)
    s_q = MLA_SCALE * jnp.concatenate([z(MLA_NOPE), sin, sin, z(32)], axis=1)
    c_k = jnp.concatenate([cos, cos, z(96)], axis=1)
    s_k = jnp.concatenate([sin, sin, z(96)], axis=1)
    return jnp.concatenate([c_q, s_q, c_k, s_k], axis=1)


def _pick(n, prefs):
    for p in prefs:
        if n % p == 0:
            return p
    return n


def kernel(x_prompt, x_sample, c_prompt, c_sample, cache_ckv, cache_kr, cache_dk, cache_dv, page_table, w_ada, b_ada, norm_attn_g, norm_ffn_g, w_in, mla_qnorm_g, w_uq, mla_kvnorm_g, w_uk, w_uv, w_o_mla, diff_lq1, diff_lk1, diff_lq2, diff_lk2, diff_subln_g, w_o_diff, w_out, w_pq, peer_keys, peer_u, peer_v, norm_final_g):
    D = D_MODEL
    nb, s_len, _ = x_prompt.shape
    n_seq, dec_seq, _ = x_sample.shape
    assert dec_seq == 1
    depth = w_ada.shape[0]
    n_pages = page_table.shape[1]
    past_len = n_pages * PAGE_SIZE
    t_p = nb * s_len

    tab_p = _rope_table(jnp.arange(s_len))
    tab_s = _rope_table(jnp.full((n_seq,), past_len, jnp.int32))
    tm_in = _pick(s_len, (256, 128))
    tq = _pick(s_len, (256, 128))
    tm_post = _pick(s_len, (256, 128))
    te = _pick(s_len, (256, 128))
    tm_peer = _pick(s_len, (512, 256, 128))
    pps = _pick(n_pages, (16, 8, 4, 2, 1))

    xp = x_prompt
    xs = x_sample.reshape(1, n_seq, D)
    c_all = jnp.concatenate([c_prompt, c_sample], axis=0)
    new_p, new_s = ([], [], [], []), ([], [], [], [])
    yp = ys = None
    for l in range(depth):
        lam_init = 0.8 - 0.6 * math.exp(-0.3 * l)
        lam = (jnp.exp(jnp.sum(diff_lq1[l] * diff_lk1[l])) - jnp.exp(jnp.sum(diff_lq2[l] * diff_lk2[l]))
               + lam_init).astype(F32).reshape(1)
        win, wq, wkv, w_abs, w_uvbd = _prep_layer(w_in[l], w_uq[l], w_uk[l], w_uv[l])
        wom, wod, wout = w_o_mla[l].astype(BF16), w_o_diff[l].astype(BF16), w_out[l].astype(BF16)
        wpq = w_pq[l].astype(BF16)
        keys = peer_keys[l].reshape(2 * PEER_HEADS, PEER_NKEYS, PEER_DKEY // 2).astype(BF16)
        pu = peer_u[l].astype(BF16)
        pvt = peer_v[l].T.astype(BF16)
        last = l == depth - 1
        g_fin = norm_final_g if last else jnp.ones((D,), F32)

        mod = _small_mm(c_all, w_ada[l], b_ada[l], pre_silu=True, tn=1024)
        mod_p = mod[:nb].reshape(nb, 1, 6 * D)
        mod_s = mod[nb:].reshape(1, n_seq, 6 * D)

        def mixer(x, modx, tab, tm):
            return _inproj(x, modx, tab, norm_attn_g[l], win, mla_qnorm_g[l], wq, mla_kvnorm_g[l], wkv, tm)

        def ffn(om, od, gates, x2d, modx, tm_a, tm_b, te_, tiles_per_row):
            x1, h2t, st = _post(om, od, gates, x2d, modx, wom, wod, wout, norm_ffn_g[l], wpq, keys, tm_a,
                                max(tiles_per_row * tm_b // tm_a, 1))
            sel = _peer_select(st, te_)
            return _peer_dense(h2t, sel, pu, pvt, x1, modx, g_fin, tm_b, tiles_per_row)

        (ckv, kr, dk, dv, q, k, vlo, vhi, dq, dkb, dvb, gates) = mixer(xp, mod_p, tab_p, tm_in)
        om, od = _prompt_attention(lam, q, k, vlo, vhi, dq, dkb, dvb, diff_subln_g[l], lam_init, tq)
        yp = ffn(om.reshape(t_p, 512), od.reshape(t_p, 1024), gates.reshape(t_p, 2048), xp.reshape(t_p, D),
                 mod_p, tm_post, tm_peer, te, s_len // tm_peer)
        for lst, r in zip(new_p, (ckv, kr, dk.reshape(nb, s_len, DIFF_KV_HEADS, 2 * DIFF_QK),
                                  dv.reshape(nb, s_len, DIFF_KV_HEADS, DIFF_V))):
            lst.append(r)

        (ckv_s, kr_s, dk_s, dv_s, q_s, _, _, _, dq_s, _, _, gates_s) = mixer(xs, mod_s, tab_s, n_seq)
        qlat = _small_mm(q_s[0].astype(F32), w_abs, tn=1024).reshape(n_seq, MLA_HEADS, KV_LORA + LANES)
        dq4 = dq_s[0].astype(F32).reshape(n_seq, DIFF_KV_HEADS, DIFF_GROUP, 2, DIFF_QK)
        zq = jnp.zeros_like(dq4[:, :, :, 0])
        qd = jnp.concatenate([jnp.concatenate([dq4[:, :, :, 0], zq], axis=-1),
                              jnp.concatenate([zq, dq4[:, :, :, 1]], axis=-1)],
                             axis=2).reshape(n_seq, DIFF_KV_HEADS * MLA_HEADS, LANES)
        olat, od_s = _decode_attention(
            page_table, lam, qlat, qd, ckv_s.reshape(n_seq, 1, KV_LORA), kr_s.reshape(n_seq, 1, MLA_ROPE),
            dk_s.reshape(n_seq, 1, 256), dv_s.reshape(n_seq, 1, 256), diff_subln_g[l],
            cache_ckv, cache_kr, cache_dk, cache_dv, l, lam_init, pps)
        om_s = _small_mm(olat.reshape(n_seq, MLA_HEADS * KV_LORA), w_uvbd).astype(BF16)
        ys = ffn(om_s, od_s.reshape(n_seq, 1024).astype(BF16), gates_s[0], xs[0], mod_s, n_seq, n_seq, n_seq, 1)
        for lst, r in zip(new_s, (ckv_s.reshape(n_seq, 1, KV_LORA), kr_s.reshape(n_seq, 1, MLA_ROPE),
                                  dk_s.reshape(n_seq, 1, DIFF_KV_HEADS, 2 * DIFF_QK),
                                  dv_s.reshape(n_seq, 1, DIFF_KV_HEADS, DIFF_V))):
            lst.append(r)
        if not last:
            raise NotImplementedError("multi-layer stacking needs the un-normalised residual stream")

    y_prompt = yp.reshape(nb, s_len, D)
    y_sample = ys.reshape(n_seq, 1, D)
    stack = lambda t: jnp.stack(t, axis=0)
    return (y_prompt, y_sample, *[stack(t) for t in new_p], *[stack(t) for t in new_s])
```

```python
import functools
import math

import jax
import jax.numpy as jnp
from jax import lax
from jax.experimental import pallas as pl
from jax.experimental.pallas import tpu as pltpu

D_MODEL = 1024
PAGE_SIZE = 128
MLA_HEADS = 8
MLA_NOPE = 64
MLA_ROPE = 32
MLA_V = 64
Q_LORA = 384
KV_LORA = 256
ROPE_THETA = 10000.0
MLA_SCALE = (MLA_NOPE + MLA_ROPE) ** -0.5
DIFF_HEADS = 8
DIFF_KV_HEADS = 2
DIFF_GROUP = DIFF_HEADS // DIFF_KV_HEADS
DIFF_QK = 64
DIFF_V = 2 * DIFF_QK
PEER_HEADS = 8
PEER_NKEYS = 128
PEER_N = PEER_NKEYS * PEER_NKEYS
PEER_DKEY = 256
PEER_TOPK = 16
RMS_EPS = 1e-6

LANES = 128
VMEM_LIMIT = 56 * 1024 * 1024
IN_PAD = 4480
NEG_INF = float("-inf")

BF16 = jnp.bfloat16
F32 = jnp.float32


def _params(*sem):
    return pltpu.CompilerParams(dimension_semantics=sem, vmem_limit_bytes=VMEM_LIMIT)


def _dot(a, b):
    return jnp.dot(a, b, preferred_element_type=F32)


def _dot_nt(a, b):
    return lax.dot_general(a, b, (((1,), (1,)), ((), ())), preferred_element_type=F32)


def _rms(x, g):
    return x * lax.rsqrt(jnp.mean(x * x, axis=-1, keepdims=True) + RMS_EPS) * g


def _small_mm_kernel(a_ref, w_ref, b_ref, o_ref, *, pre_silu):
    a = a_ref[...]
    if pre_silu:
        a = a * jax.nn.sigmoid(a)
    o_ref[...] = _dot(a.astype(BF16), w_ref[...].astype(BF16)) + b_ref[...]


def _small_mm(a, w, bias=None, pre_silu=False, tn=512):
    m, k = a.shape
    n = w.shape[1]
    tn = min(tn, n)
    if bias is None:
        bias = jnp.zeros((n,), F32)
    return pl.pallas_call(
        functools.partial(_small_mm_kernel, pre_silu=pre_silu),
        grid=(n // tn,),
        in_specs=[pl.BlockSpec((m, k), lambda j: (0, 0)),
                  pl.BlockSpec((k, tn), lambda j: (0, j)),
                  pl.BlockSpec((1, tn), lambda j: (0, j))],
        out_specs=pl.BlockSpec((m, tn), lambda j: (0, j)),
        out_shape=jax.ShapeDtypeStruct((m, n), F32),
        compiler_params=_params("arbitrary"),
        name="small_mm",
    )(a, w, bias.reshape(1, n))


C_CQ, C_CKV, C_DQ, C_DK, C_DV, C_GA, C_GB, C_KR, C_KRR = 0, 384, 640, 1664, 1920, 2176, 3200, 4224, 4352


def _inproj_kernel(x_ref, mod_ref, tab_ref, gattn_ref, win_ref, qng_ref, wq_ref, kvg_ref, wkv_ref,
                   ckv_ref, kr_ref, dk_ref, dv_ref, q_ref, k_ref, vlo_ref, vhi_ref, dq_ref,
                   dkb_ref, dvb_ref, gates_ref):
    D = D_MODEL
    x = x_ref[0]
    mod = mod_ref[0]
    h = _rms(x, gattn_ref[...]) * (1.0 + mod[:, D:2 * D]) + mod[:, 0:D]
    hb = h.astype(BF16)
    tab = tab_ref[...]
    c_q, s_q, c_k, s_k = (tab[:, i * LANES:(i + 1) * LANES] for i in range(4))

    def proj(lo, hi):
        return _dot(hb, win_ref[:, lo:hi])

    gates_ref[0] = jax.nn.sigmoid(proj(C_GA, C_KR))
    dq_ref[0] = (proj(C_DQ, C_DK) * (DIFF_QK ** -0.5)).astype(BF16)
    dk = proj(C_DK, C_DV)
    dv = proj(C_DV, C_GA)
    dk_ref[0] = dk
    dv_ref[0] = dv
    dkb_ref[0] = dk.astype(BF16)
    dvb_ref[0] = dv.astype(BF16)

    kr2 = proj(C_KR, IN_PAD)
    kr = kr2[:, :LANES] * c_k + kr2[:, LANES:] * s_k
    kr_ref[0] = kr[:, :MLA_ROPE]
    kr_sh = pltpu.roll(kr, MLA_NOPE, 1)

    ckv = _rms(proj(C_CKV, C_DQ), kvg_ref[...])
    ckv_ref[0] = ckv
    kv = _dot(ckv.astype(BF16), wkv_ref[...])
    for hd in range(MLA_HEADS):
        sl = slice(hd * LANES, (hd + 1) * LANES)
        k_ref[0, :, sl] = (kv[:, sl] + kr_sh).astype(BF16)
    vlo_ref[0] = kv[:, 1024:1536].astype(BF16)
    vhi_ref[0] = kv[:, 1536:2048].astype(BF16)

    cq = _rms(proj(C_CQ, C_CKV), qng_ref[...]).astype(BF16)
    q2 = _dot(cq, wq_ref[...])
    for hd in range(MLA_HEADS):
        sl = slice(hd * LANES, (hd + 1) * LANES)
        sr = slice(1024 + hd * LANES, 1024 + (hd + 1) * LANES)
        q_ref[0, :, sl] = (q2[:, sl] * c_q + q2[:, sr] * s_q).astype(BF16)


def _inproj(x, mod, tab, g_attn, win, qn_g, wq, kvn_g, wkv, tm):
    nb, s, D = x.shape
    mrows = mod.shape[1]
    grid = (nb, s // tm)
    tok = lambda w, dt: jax.ShapeDtypeStruct((nb, s, w), dt)
    tspec = lambda w: pl.BlockSpec((1, tm, w), lambda b, i: (b, i, 0))
    const = lambda shape: pl.BlockSpec(shape, lambda b, i: (0,) * len(shape))
    mod_spec = (pl.BlockSpec((1, 1, 6 * D), lambda b, i: (b, 0, 0)) if mrows == 1
                else pl.BlockSpec((1, tm, 6 * D), lambda b, i: (b, i, 0)))
    widths = [(KV_LORA, F32), (MLA_ROPE, F32), (256, F32), (256, F32), (1024, BF16), (1024, BF16),
              (512, BF16), (512, BF16), (1024, BF16), (256, BF16), (256, BF16), (2048, F32)]
    return pl.pallas_call(
        _inproj_kernel,
        grid=grid,
        in_specs=[tspec(D), mod_spec, pl.BlockSpec((tm, 4 * LANES), lambda b, i: (i, 0)),
                  const((1, D)), const((D, IN_PAD)), const((1, Q_LORA)), const((Q_LORA, 2048)),
                  const((1, KV_LORA)), const((KV_LORA, 2048))],
        out_specs=[tspec(w) for w, _ in widths],
        out_shape=[tok(w, dt) for w, dt in widths],
        compiler_params=_params("parallel", "arbitrary"),
        name="inproj",
    )(x, mod, tab, g_attn.reshape(1, D), win, qn_g.reshape(1, Q_LORA), wq, kvn_g.reshape(1, KV_LORA), wkv)


N_CHAIN = MLA_HEADS + 2 * DIFF_HEADS
CHAIN_GROUPS = ((tuple(range(MLA_HEADS)), tuple(range(DIFF_HEADS))),)


def _softmax_update(s, v, m_ref, l_ref, acc_ref):
    reps = s.shape[1] // LANES
    m_prev = m_ref[...]
    m_next = jnp.maximum(m_prev, jnp.max(s, axis=1, keepdims=True))
    p = jnp.exp(s - jnp.tile(m_next, (1, reps)))
    alpha = jnp.exp(m_prev - m_next)
    l_ref[...] = alpha * l_ref[...] + jnp.sum(p, axis=1, keepdims=True)
    m_ref[...] = m_next
    acc_ref[...] = alpha * acc_ref[...] + _dot(p.astype(BF16), v)


def _attn_kernel(lam_ref, q_ref, k_ref, vlo_ref, vhi_ref, dq_ref, dk_ref, dv_ref, subg_ref,
                 om_ref, od_ref, qd_ref, m_ref, l_ref, acc_ref, *, tq, tk, lam_init):
    q0 = pl.program_id(1) * tq
    n_full = q0 // tk
    lam = lam_ref[0]

    m_ref[...] = jnp.full(m_ref.shape, NEG_INF, F32)
    l_ref[...] = jnp.zeros(l_ref.shape, F32)
    acc_ref[...] = jnp.zeros(acc_ref.shape, F32)
    lane = lax.broadcasted_iota(jnp.int32, (tq, LANES), 1)
    for hd in range(DIFF_HEADS):
        qd = dq_ref[0, :, hd * LANES:(hd + 1) * LANES]
        zero = jnp.zeros_like(qd)
        qd_ref[2 * hd] = jnp.where(lane < DIFF_QK, qd, zero)
        qd_ref[2 * hd + 1] = jnp.where(lane >= DIFF_QK, qd, zero)

    col = lax.broadcasted_iota(jnp.int32, (tq, tk), 1)
    row = lax.broadcasted_iota(jnp.int32, (tq, tk), 0)

    def chunk(j, diagonal, mla_heads, diff_heads):
        off = pl.multiple_of(j * tk, tk)
        rows = lambda ref, lane0: ref[0, pl.ds(off, tk), lane0:lane0 + LANES]
        rel = col - row + (off - q0)
        for hd in mla_heads:
            s = _dot_nt(q_ref[0, :, hd * LANES:(hd + 1) * LANES], rows(k_ref, hd * LANES))
            if diagonal:
                s = jnp.where(rel <= 0, s, NEG_INF)
            v = rows(vlo_ref if hd % 2 == 0 else vhi_ref, (hd // 2) * LANES)
            _softmax_update(s, v, m_ref.at[hd], l_ref.at[hd], acc_ref.at[hd])
        dist = rel.astype(F32)
        if diagonal:
            dist = jnp.where(rel <= 0, dist, NEG_INF)
        for hd in diff_heads:
            kv = hd // DIFF_GROUP
            bias = dist * (2.0 ** (-8.0 * (hd + 1) / DIFF_HEADS))
            kk = rows(dk_ref, kv * LANES)
            vv = rows(dv_ref, kv * LANES)
            for mp in range(2):
                c = MLA_HEADS + 2 * hd + mp
                s = _dot_nt(qd_ref[2 * hd + mp], kk) + bias
                _softmax_update(s, vv, m_ref.at[c], l_ref.at[c], acc_ref.at[c])

    for mla_heads, diff_heads in CHAIN_GROUPS:
        def body(j, carry, mla_heads=mla_heads, diff_heads=diff_heads):
            chunk(j, False, mla_heads, diff_heads)
            return carry

        lax.fori_loop(0, n_full, body, 0)
        chunk(n_full, True, mla_heads, diff_heads)

    inv = lambda c: acc_ref[c] / l_ref[c]
    for pair in range(MLA_HEADS // 2):
        om_ref[0, :, pair * LANES:(pair + 1) * LANES] = (inv(2 * pair) + inv(2 * pair + 1)).astype(BF16)
    for hd in range(DIFF_HEADS):
        c = MLA_HEADS + 2 * hd
        o = inv(c) - lam * inv(c + 1)
        od_ref[0, :, hd * LANES:(hd + 1) * LANES] = (_rms(o, subg_ref[...]) * (1.0 - lam_init)).astype(BF16)


def _prompt_attention(lam, q, k, vlo, vhi, dq, dkb, dvb, subln_g, lam_init, tq):
    nb, s, _ = q.shape
    qspec = lambda w: pl.BlockSpec((1, tq, w), lambda b, i: (b, i, 0))
    full = lambda w: pl.BlockSpec((1, s, w), lambda b, i: (b, 0, 0))
    state = pltpu.VMEM((N_CHAIN, tq, LANES), F32)
    return pl.pallas_call(
        functools.partial(_attn_kernel, tq=tq, tk=tq, lam_init=lam_init),
        grid=(nb, s // tq),
        in_specs=[pl.BlockSpec(memory_space=pltpu.SMEM), qspec(1024), full(1024), full(512), full(512),
                  qspec(1024), full(256), full(256), pl.BlockSpec((1, DIFF_V), lambda b, i: (0, 0))],
        out_specs=[qspec(512), qspec(1024)],
        out_shape=[jax.ShapeDtypeStruct((nb, s, 512), BF16), jax.ShapeDtypeStruct((nb, s, 1024), BF16)],
        scratch_shapes=[pltpu.VMEM((2 * DIFF_HEADS, tq, LANES), BF16), state, state, state],
        compiler_params=_params("parallel", "arbitrary"),
        name="prompt_attn",
    )(lam, q, k, vlo, vhi, dq, dkb, dvb, subln_g.reshape(1, DIFF_V))


N_CACHE = 4
PAGE_LOOKAHEAD = 2
PAGE_SLOTS = PAGE_LOOKAHEAD + 1


def _decode_kernel(pt_ref, lam_ref, qlat_ref, qd_ref, nckv_ref, nkr_ref, ndk_ref, ndv_ref, subg_ref,
                   ckv_hbm, krt_hbm, dk_hbm, dv_hbm, olat_ref, od_ref,
                   m_m, l_m, a_m, m_d, l_d, a_d, ckv_buf, krt_buf, dk_buf, dv_buf, sem,
                   *, pps, past_len, lam_init, layer):
    seq = pl.program_id(0)
    step = pl.program_id(1)
    n_steps = pl.num_programs(1)
    R = MLA_HEADS
    caches = (ckv_hbm, krt_hbm, dk_hbm, dv_hbm)
    bufs = (ckv_buf, krt_buf, dk_buf, dv_buf)

    def page_copies(q_seq, q_step, q_slot):
        copies = []
        for j in range(pps):
            page = pt_ref[q_seq, q_step * pps + j]
            for k in range(N_CACHE):
                copies.append(pltpu.make_async_copy(caches[k].at[layer, page], bufs[k].at[q_slot, j],
                                                    sem.at[q_slot, k]))
        return copies

    g = seq * n_steps + step
    total = pl.num_programs(0) * n_steps
    slot = g % PAGE_SLOTS

    def fetch(q):
        for cp in page_copies(q // n_steps, q % n_steps, q % PAGE_SLOTS):
            cp.start()

    for q in range(PAGE_LOOKAHEAD):
        @pl.when((g == 0) & (q < total))
        def _(q=q):
            fetch(q)

    @pl.when(g + PAGE_LOOKAHEAD < total)
    def _():
        fetch(g + PAGE_LOOKAHEAD)

    for cp in page_copies(seq, step, slot):
        cp.wait()
    pages = [bufs[k].at[slot, j] for j in range(pps) for k in range(N_CACHE)]
    RD = DIFF_KV_HEADS * R
    W2 = DIFF_KV_HEADS * PAGE_SIZE

    qlat = qlat_ref[0]
    qd = qd_ref[0]
    kv_row = lax.broadcasted_iota(jnp.int32, (RD, LANES), 0) // R

    @pl.when(step == 0)
    def _():
        s = (jnp.sum(qlat[:, :KV_LORA] * nckv_ref[0], axis=-1, keepdims=True)
             + jnp.sum(qlat[:, KV_LORA:KV_LORA + MLA_ROPE] * nkr_ref[0], axis=-1, keepdims=True))
        m_m[...] = jnp.broadcast_to(s, (R, LANES))
        l_m[...] = jnp.ones((R, LANES), F32)
        a_m[...] = jnp.broadcast_to(nckv_ref[0], (R, KV_LORA))
        ndk, ndv = ndk_ref[0], ndv_ref[0]
        own = lambda t: jnp.where(kv_row == 0, jnp.broadcast_to(t[:, :LANES], (RD, LANES)),
                                  jnp.broadcast_to(t[:, LANES:], (RD, LANES)))
        m_d[...] = jnp.broadcast_to(jnp.sum(qd * own(ndk), axis=-1, keepdims=True), (RD, LANES))
        l_d[...] = jnp.ones((RD, LANES), F32)
        a_d[...] = own(ndv)

    def update(s_list, v_list, m_ref, l_ref, a_ref):
        m_old = m_ref[...]
        mx = s_list[0]
        for s in s_list[1:]:
            mx = jnp.maximum(mx, s)
        m_new = jnp.maximum(m_old, jnp.max(mx, axis=-1, keepdims=True))
        alpha = jnp.exp(m_old - m_new)
        reps = s_list[0].shape[1] // LANES
        m_b = jnp.tile(m_new, (1, reps))
        psum = None
        pv = None
        for s, v in zip(s_list, v_list):
            p = jnp.exp(s - m_b)
            psum = p if psum is None else psum + p
            d = _dot(p.astype(BF16), v)
            pv = d if pv is None else pv + d
        m_ref[...] = m_new
        l_ref[...] = alpha * l_ref[...] + jnp.sum(psum, axis=-1, keepdims=True)
        a_ref[...] = jnp.tile(alpha, (1, a_ref.shape[1] // LANES)) * a_ref[...] + pv

    qc = qlat[:, :KV_LORA].astype(BF16)
    qr = qlat[:, KV_LORA:KV_LORA + MLA_ROPE].astype(BF16)
    ckvs = [pages[4 * j][...].astype(BF16) for j in range(pps)]
    s_m = [_dot_nt(qc, ckvs[j]) + _dot(qr, pages[4 * j + 1][...].astype(BF16)) for j in range(pps)]
    update(s_m, ckvs, m_m, l_m, a_m)

    col = lax.broadcasted_iota(jnp.int32, (RD, W2), 1)
    row = lax.broadcasted_iota(jnp.int32, (RD, W2), 0)
    hd = (row // R) * DIFF_GROUP + row % DIFF_GROUP
    slope = jnp.exp2(-8.0 * (hd + 1).astype(F32) / DIFF_HEADS)
    base = jnp.where(col % DIFF_KV_HEADS == row // R, slope * (col // DIFF_KV_HEADS).astype(F32), NEG_INF)
    qdb = qd.astype(BF16)
    s_d, v_d = [], []
    for j in range(pps):
        dist0 = ((step * pps + j) * PAGE_SIZE - past_len).astype(F32)
        s_d.append(_dot_nt(qdb, pages[4 * j + 2][...].astype(BF16)) + (base + slope * dist0))
        v_d.append(pages[4 * j + 3][...].astype(BF16))
    update(s_d, v_d, m_d, l_d, a_d)

    @pl.when(step == n_steps - 1)
    def _():
        olat_ref[0] = a_m[...] / jnp.tile(l_m[...], (1, KV_LORA // LANES))
        lam = lam_ref[0]
        o2 = a_d[...] / l_d[...]
        for kv in range(DIFF_KV_HEADS):
            o = o2[kv * R:kv * R + DIFF_GROUP] - lam * o2[kv * R + DIFF_GROUP:(kv + 1) * R]
            od_ref[0, kv] = _rms(o, subg_ref[...]) * (1.0 - lam_init)


def _decode_attention(page_table, lam, qlat, qd, nckv, nkr, ndk, ndv, subln_g, cache_ckv, cache_kr,
                      cache_dk, cache_dv, layer, lam_init, pps):
    n_seq, n_pages = page_table.shape
    past_len = n_pages * PAGE_SIZE
    R = MLA_HEADS
    RD = DIFF_KV_HEADS * R
    W2 = DIFF_KV_HEADS * PAGE_SIZE
    n_layers, n_pool = cache_ckv.shape[:2]
    cache_krt = jnp.swapaxes(cache_kr, 2, 3)
    cache_dk2 = cache_dk.reshape(n_layers, n_pool, W2, 2 * DIFF_QK)
    cache_dv2 = cache_dv.reshape(n_layers, n_pool, W2, DIFF_V)
    seq3 = lambda w: pl.BlockSpec((1, 1, w), lambda n, s, pt: (n, 0, 0))
    in_specs = [pl.BlockSpec(memory_space=pltpu.SMEM),
                pl.BlockSpec((1, R, 384), lambda n, s, pt: (n, 0, 0)),
                pl.BlockSpec((1, RD, LANES), lambda n, s, pt: (n, 0, 0)),
                seq3(KV_LORA), seq3(MLA_ROPE), seq3(256), seq3(256),
                pl.BlockSpec((1, DIFF_V), lambda n, s, pt: (0, 0))]
    args = [lam, qlat, qd, nckv, nkr, ndk, ndv, subln_g.reshape(1, DIFF_V),
            cache_ckv, cache_krt, cache_dk2, cache_dv2]
    in_specs += [pl.BlockSpec(memory_space=pl.ANY)] * N_CACHE
    page_buf = lambda rows, width: pltpu.VMEM((PAGE_SLOTS, pps, rows, width), F32)
    grid_spec = pltpu.PrefetchScalarGridSpec(
        num_scalar_prefetch=1,
        grid=(n_seq, n_pages // pps),
        in_specs=in_specs,
        out_specs=[pl.BlockSpec((1, R, KV_LORA), lambda n, s, pt: (n, 0, 0)),
                   pl.BlockSpec((1, DIFF_KV_HEADS, DIFF_GROUP, LANES), lambda n, s, pt: (n, 0, 0, 0))],
        scratch_shapes=[pltpu.VMEM((R, LANES), F32), pltpu.VMEM((R, LANES), F32), pltpu.VMEM((R, KV_LORA), F32),
                        pltpu.VMEM((RD, LANES), F32), pltpu.VMEM((RD, LANES), F32), pltpu.VMEM((RD, LANES), F32),
                        page_buf(PAGE_SIZE, KV_LORA), page_buf(MLA_ROPE, PAGE_SIZE),
                        page_buf(W2, 2 * DIFF_QK), page_buf(W2, DIFF_V),
                        pltpu.SemaphoreType.DMA((PAGE_SLOTS, N_CACHE))],
    )
    return pl.pallas_call(
        functools.partial(_decode_kernel, pps=pps, past_len=past_len, lam_init=lam_init, layer=layer),
        grid_spec=grid_spec,
        out_shape=[jax.ShapeDtypeStruct((n_seq, R, KV_LORA), F32),
                   jax.ShapeDtypeStruct((n_seq, DIFF_KV_HEADS, DIFF_GROUP, LANES), F32)],
        compiler_params=_params("arbitrary", "arbitrary"),
        name="decode_attn",
    )(page_table, *args)


def _post_kernel(om_ref, od_ref, gates_ref, x_ref, mod_ref, wom_ref, wod_ref, wout_ref, gffn_ref, wpq_ref,
                 keys_ref, x1_ref, h2t_ref, st_ref):
    D = D_MODEL
    mod = mod_ref[0]
    gates = gates_ref[...]
    merged = gates[:, :D] * _dot(om_ref[...], wom_ref[...]) + gates[:, D:] * _dot(od_ref[...], wod_ref[...])
    x1 = x_ref[...] + mod[:, 2 * D:3 * D] * _dot(merged.astype(BF16), wout_ref[...])
    x1_ref[...] = x1
    h2 = _rms(x1, gffn_ref[...]) * (1.0 + mod[:, 4 * D:5 * D]) + mod[:, 3 * D:4 * D]
    h2t_ref[...] = h2.T.astype(BF16)
    pq = _dot(h2.astype(BF16), wpq_ref[...]).astype(BF16)
    for i in range(2 * PEER_HEADS):
        st_ref[i] = _dot_nt(keys_ref[i], pq[:, i * LANES:(i + 1) * LANES])


def _post(om, od, gates, x, mod, wom, wod, wout, g_ffn, wpq, keys, tm, tiles_per_row):
    t, D = x.shape
    mrows = mod.shape[1]
    tok = lambda w: pl.BlockSpec((tm, w), lambda i: (i, 0))
    const = lambda shape: pl.BlockSpec(shape, lambda i: (0,) * len(shape))
    mod_spec = (pl.BlockSpec((1, 1, 6 * D), lambda i: (i // tiles_per_row, 0, 0)) if mrows == 1
                else pl.BlockSpec((1, tm, 6 * D), lambda i: (0, i, 0)))
    nk = 2 * PEER_HEADS
    return pl.pallas_call(
        _post_kernel,
        grid=(t // tm,),
        in_specs=[tok(512), tok(1024), tok(2048), tok(D), mod_spec, const((512, D)), const((1024, D)),
                  const((D, D)), const((1, D)), const((D, PEER_HEADS * PEER_DKEY)),
                  const((nk, PEER_NKEYS, PEER_DKEY // 2))],
        out_specs=[tok(D), pl.BlockSpec((D, tm), lambda i: (0, i)),
                   pl.BlockSpec((nk, PEER_NKEYS, tm), lambda i: (0, 0, i))],
        out_shape=[jax.ShapeDtypeStruct((t, D), F32), jax.ShapeDtypeStruct((D, t), BF16),
                   jax.ShapeDtypeStruct((nk, PEER_NKEYS, t), F32)],
        compiler_params=_params("parallel"),
        name="post_attn",
    )(om, od, gates, x, mod, wom, wod, wout, g_ffn.reshape(1, D), wpq, keys)


N_EXT = PEER_TOPK + 1
CAND = [(i, j) for i in range(N_EXT) for j in range(N_EXT) if (i + 1) * (j + 1) <= N_EXT]


def _extract_desc(s, n):
    out = []
    for _ in range(n):
        m = jnp.max(s, axis=0, keepdims=True)
        out.append(m)
        s = jnp.where(s == m, NEG_INF, s)
    return out


CAND_ROWS = -(-len(CAND) // 8) * 8
PEER_CA = 16


def _select_kernel(st_ref, e2_ref, te_ref, cc_ref, cand_ref):
    te = st_ref.shape[-1]
    cand_ref[...] = jnp.full(cand_ref.shape, NEG_INF, F32)

    def head(h, carry):
        s1 = st_ref[2 * h]
        s2 = st_ref[2 * h + 1]
        t1 = _extract_desc(s1, N_EXT)
        t2 = _extract_desc(s2, N_EXT)
        for r, (i, j) in enumerate(CAND):
            cand_ref[r:r + 1, :] = t1[i] + t2[j]
        top = _extract_desc(cand_ref[...], N_EXT)
        z = jnp.ones((1, te), F32)
        for c in top[1:PEER_TOPK]:
            z = z + jnp.exp(c - top[0])
        e2_ref[h] = jnp.exp(s2 - t2[0])
        thr_e = jnp.exp((0.5 * (top[PEER_TOPK - 1] + top[PEER_TOPK]) - s1) - t2[0])
        cc = jnp.exp(s1 - t1[0]) * (0.5 / z)
        for k in range(PEER_NKEYS // PEER_CA):
            te_ref[h, k] = thr_e[k * PEER_CA:(k + 1) * PEER_CA]
            cc_ref[h, k] = cc[k * PEER_CA:(k + 1) * PEER_CA]
        return carry

    lax.fori_loop(0, PEER_HEADS, head, 0)


def _peer_select(st, te):
    nk, n, t = st.shape
    nck = n // PEER_CA
    spec4 = pl.BlockSpec((PEER_HEADS, nck, PEER_CA, te), lambda i: (0, 0, 0, i))
    shape4 = jax.ShapeDtypeStruct((PEER_HEADS, nck, PEER_CA, t), F32)
    return pl.pallas_call(
        _select_kernel,
        grid=(t // te,),
        in_specs=[pl.BlockSpec((nk, n, te), lambda i: (0, 0, i))],
        out_specs=[pl.BlockSpec((PEER_HEADS, n, te), lambda i: (0, 0, i)), spec4, spec4],
        out_shape=[jax.ShapeDtypeStruct((PEER_HEADS, n, t), F32), shape4, shape4],
        scratch_shapes=[pltpu.VMEM((CAND_ROWS, te), F32)],
        compiler_params=_params("parallel"),
        name="peer_select",
    )(st)


SUB = 8
PAIR = 2 * PEER_NKEYS


def _peer_kernel(h2t_ref, e2_ref, te_ref, cc_ref, pu_ref, pvt_ref, x1_ref, mod_ref, gfin_ref, y_ref,
                 acc_ref, ut_ref, wt_ref, *, ca):
    D = D_MODEL
    c = pl.program_id(1)
    tm = h2t_ref.shape[1]
    n_pair = ca // 2

    @pl.when(c == 0)
    def _():
        acc_ref[...] = jnp.zeros_like(acc_ref)

    def first_matmul(p):
        ut_ref[p % 2] = _dot(pu_ref[p * PAIR:(p + 1) * PAIR, :], h2t_ref[...])

    def gate_rows(p, ah):
        al = 2 * p + ah
        row_of = lambda ref, h, lanes: jnp.broadcast_to(ref[h, c, al:al + 1, lanes], (SUB, LANES))
        for tg in range(tm // LANES):
            lanes = slice(tg * LANES, (tg + 1) * LANES)
            tes = [row_of(te_ref, h, lanes) for h in range(PEER_HEADS)]
            ccs = [row_of(cc_ref, h, lanes) for h in range(PEER_HEADS)]
            for bp in range(PEER_NKEYS // (2 * SUB)):
                ws = []
                for r0 in (2 * SUB * bp, 2 * SUB * bp + SUB):
                    g = None
                    for h in range(PEER_HEADS):
                        e2 = e2_ref[h, r0:r0 + SUB, lanes]
                        t = jnp.where(e2 > tes[h], e2 * ccs[h], 0.0)
                        g = t if g is None else g + t
                    u = ut_ref[p % 2, ah * PEER_NKEYS + r0:ah * PEER_NKEYS + r0 + SUB, lanes]
                    ws.append((g * u) * (1.0 + lax.erf(u * math.sqrt(0.5))))
                w0 = p * PAIR + ah * PEER_NKEYS + 2 * SUB * bp
                wt_ref[w0:w0 + 2 * SUB, lanes] = jnp.concatenate(ws, axis=0).astype(BF16)

    first_matmul(0)
    for p in range(n_pair):
        if p + 1 < n_pair:
            first_matmul(p + 1)
        gate_rows(p, 0)
        gate_rows(p, 1)
        if p % 2 == 1:
            lo, hi = (p - 1) * PAIR, (p + 1) * PAIR
            acc_ref[...] += _dot(pvt_ref[:, lo:hi], wt_ref[lo:hi, :])

    @pl.when(c == pl.num_programs(1) - 1)
    def _():
        mod = mod_ref[0]
        x2 = x1_ref[...] + mod[:, 5 * D:6 * D] * acc_ref[...].T
        y_ref[...] = _rms(x2, gfin_ref[...])


def _peer_dense(h2t, sel, pu, pvt, x1, mod, g_final, tm, tiles_per_row):
    t, D = x1.shape
    mrows = mod.shape[1]
    ca = PEER_CA
    nch = ca * PEER_NKEYS
    e2, te, cc = sel
    mod_spec = (pl.BlockSpec((1, 1, 6 * D), lambda i, c: (i // tiles_per_row, 0, 0)) if mrows == 1
                else pl.BlockSpec((1, tm, 6 * D), lambda i, c: (0, i, 0)))
    row_spec = pl.BlockSpec((PEER_HEADS, PEER_NKEYS // ca, ca, tm), lambda i, c: (0, 0, 0, i))
    return pl.pallas_call(
        functools.partial(_peer_kernel, ca=ca),
        grid=(t // tm, PEER_N // nch),
        in_specs=[pl.BlockSpec((D, tm), lambda i, c: (0, i)),
                  pl.BlockSpec((PEER_HEADS, PEER_NKEYS, tm), lambda i, c: (0, 0, i)),
                  row_spec, row_spec,
                  pl.BlockSpec((nch, D), lambda i, c: (c, 0)),
                  pl.BlockSpec((D, nch), lambda i, c: (0, c)),
                  pl.BlockSpec((tm, D), lambda i, c: (i, 0)),
                  mod_spec,
                  pl.BlockSpec((1, D), lambda i, c: (0, 0))],
        out_specs=pl.BlockSpec((tm, D), lambda i, c: (i, 0)),
        out_shape=jax.ShapeDtypeStruct((t, D), F32),
        scratch_shapes=[pltpu.VMEM((D, tm), F32), pltpu.VMEM((2, PAIR, tm), F32), pltpu.VMEM((nch, tm), BF16)],
        compiler_params=_params("parallel", "arbitrary"),
        name="peer_dense",
    )(h2t, e2, te, cc, pu, pvt, x1, mod, g_final.reshape(1, D))


def _rot_half(w):
    half = MLA_ROPE // 2
    return jnp.concatenate([-w[..., half:], w[..., :half]], axis=-1)


def _prep_layer(w_in, w_uq, w_uk, w_uv):
    D = D_MODEL
    o_kr = Q_LORA + KV_LORA
    w_kr = w_in[:, o_kr:o_kr + MLA_ROPE]
    zpad = jnp.zeros((D, LANES - MLA_ROPE), F32)
    win = jnp.concatenate([w_in[:, :o_kr], w_in[:, o_kr + MLA_ROPE:], w_kr, zpad, _rot_half(w_kr), zpad],
                          axis=1).astype(BF16)
    H = MLA_HEADS
    r = w_uq.reshape(Q_LORA, H, MLA_NOPE + MLA_ROPE)
    nope, rope = r[..., :MLA_NOPE], r[..., MLA_NOPE:]
    z32 = jnp.zeros((Q_LORA, H, LANES - MLA_NOPE - MLA_ROPE), F32)
    w_q = jnp.concatenate([nope, rope, z32], axis=-1).reshape(Q_LORA, H * LANES)
    w_qr = jnp.concatenate([jnp.zeros_like(nope), _rot_half(rope), z32], axis=-1).reshape(Q_LORA, H * LANES)
    wq = jnp.concatenate([w_q, w_qr], axis=1).astype(BF16)
    uk = w_uk.reshape(KV_LORA, H, MLA_NOPE)
    w_k = jnp.concatenate([uk, jnp.zeros_like(uk)], axis=-1).reshape(KV_LORA, H * LANES)
    uv = w_uv.reshape(KV_LORA, H // 2, 2, MLA_V)
    zv = jnp.zeros((KV_LORA, H // 2, MLA_V), F32)
    v_lo = jnp.concatenate([uv[:, :, 0], zv], axis=-1).reshape(KV_LORA, H // 2 * LANES)
    v_hi = jnp.concatenate([zv, uv[:, :, 1]], axis=-1).reshape(KV_LORA, H // 2 * LANES)
    wkv = jnp.concatenate([w_k, v_lo, v_hi], axis=1).astype(BF16)
    eye_h = jnp.eye(H, dtype=F32)
    uk_t = jnp.transpose(uk, (1, 2, 0))
    top = jnp.concatenate([uk_t, jnp.zeros((H, MLA_NOPE, LANES), F32)], axis=-1)
    mid = jnp.concatenate([jnp.zeros((H, MLA_ROPE, KV_LORA), F32),
                           jnp.broadcast_to(jnp.eye(MLA_ROPE, LANES, dtype=F32), (H, MLA_ROPE, LANES))], axis=-1)
    bot = jnp.zeros((H, LANES - MLA_NOPE - MLA_ROPE, KV_LORA + LANES), F32)
    m_h = jnp.concatenate([top, mid, bot], axis=1)
    w_abs = jnp.einsum('hrc,hg->hrgc', m_h, eye_h).reshape(H * LANES, H * (KV_LORA + LANES))
    uv_h = jnp.transpose(w_uv.reshape(KV_LORA, H, MLA_V), (1, 0, 2))
    w_uvbd = jnp.einsum('hcv,hg->hcgv', uv_h, eye_h).reshape(H * KV_LORA, H * MLA_V)
    return win, wq, wkv, w_abs, w_uvbd


def _rope_table(pos):
    inv = ROPE_THETA ** (-jnp.arange(0, MLA_ROPE, 2, dtype=F32) / MLA_ROPE)
    ang = pos.astype(F32)[:, None] * inv[None, :]
    cos, sin = jnp.cos(ang), jnp.sin(ang)
    n = pos.shape[0]
    z = lambda w: jnp.zeros((n, w), F32)
    c_q = MLA_SCALE * jnp.concatenate([jnp.ones((n, MLA_NOPE), F32), cos, cos, z(32)], axis=1)
    s_q = MLA_SCALE * jnp.concatenate([z(MLA_NOPE), sin, sin, z(32)], axis=1)
    c_k = jnp.concatenate([cos, cos, z(96)], axis=1)
    s_k = jnp.concatenate([sin, sin, z(96)], axis=1)
    return jnp.concatenate([c_q, s_q, c_k, s_k], axis=1)


def _pick(n, prefs):
    for p in prefs:
        if n % p == 0:
            return p
    return n


def kernel(x_prompt, x_sample, c_prompt, c_sample, cache_ckv, cache_kr, cache_dk, cache_dv, page_table, w_ada, b_ada, norm_attn_g, norm_ffn_g, w_in, mla_qnorm_g, w_uq, mla_kvnorm_g, w_uk, w_uv, w_o_mla, diff_lq1, diff_lk1, diff_lq2, diff_lk2, diff_subln_g, w_o_diff, w_out, w_pq, peer_keys, peer_u, peer_v, norm_final_g):
    D = D_MODEL
    nb, s_len, _ = x_prompt.shape
    n_seq, dec_seq, _ = x_sample.shape
    assert dec_seq == 1
    depth = w_ada.shape[0]
    n_pages = page_table.shape[1]
    past_len = n_pages * PAGE_SIZE
    t_p = nb * s_len

    tab_p = _rope_table(jnp.arange(s_len))
    tab_s = _rope_table(jnp.full((n_seq,), past_len, jnp.int32))
    tm_in = _pick(s_len, (256, 128))
    tq = _pick(s_len, (256, 128))
    tm_post = _pick(s_len, (256, 128))
    te = _pick(s_len, (256, 128))
    tm_peer = _pick(s_len, (512, 256, 128))
    pps = _pick(n_pages, (16, 8, 4, 2, 1))

    xp = x_prompt
    xs = x_sample.reshape(1, n_seq, D)
    c_all = jnp.concatenate([c_prompt, c_sample], axis=0)
    new_p, new_s = ([], [], [], []), ([], [], [], [])
    yp = ys = None
    for l in range(depth):
        lam_init = 0.8 - 0.6 * math.exp(-0.3 * l)
        lam = (jnp.exp(jnp.sum(diff_lq1[l] * diff_lk1[l])) - jnp.exp(jnp.sum(diff_lq2[l] * diff_lk2[l]))
               + lam_init).astype(F32).reshape(1)
        win, wq, wkv, w_abs, w_uvbd = _prep_layer(w_in[l], w_uq[l], w_uk[l], w_uv[l])
        wom, wod, wout = w_o_mla[l].astype(BF16), w_o_diff[l].astype(BF16), w_out[l].astype(BF16)
        wpq = w_pq[l].astype(BF16)
        keys = peer_keys[l].reshape(2 * PEER_HEADS, PEER_NKEYS, PEER_DKEY // 2).astype(BF16)
        pu = peer_u[l].astype(BF16)
        pvt = peer_v[l].T.astype(BF16)
        last = l == depth - 1
        g_fin = norm_final_g if last else jnp.ones((D,), F32)

        mod = _small_mm(c_all, w_ada[l], b_ada[l], pre_silu=True, tn=1024)
        mod_p = mod[:nb].reshape(nb, 1, 6 * D)
        mod_s = mod[nb:].reshape(1, n_seq, 6 * D)

        def mixer(x, modx, tab, tm):
            return _inproj(x, modx, tab, norm_attn_g[l], win, mla_qnorm_g[l], wq, mla_kvnorm_g[l], wkv, tm)

        def ffn(om, od, gates, x2d, modx, tm_a, tm_b, te_, tiles_per_row):
            x1, h2t, st = _post(om, od, gates, x2d, modx, wom, wod, wout, norm_ffn_g[l], wpq, keys, tm_a,
                                max(tiles_per_row * tm_b // tm_a, 1))
            sel = _peer_select(st, te_)
            return _peer_dense(h2t, sel, pu, pvt, x1, modx, g_fin, tm_b, tiles_per_row)

        (ckv, kr, dk, dv, q, k, vlo, vhi, dq, dkb, dvb, gates) = mixer(xp, mod_p, tab_p, tm_in)
        om, od = _prompt_attention(lam, q, k, vlo, vhi, dq, dkb, dvb, diff_subln_g[l], lam_init, tq)
        yp = ffn(om.reshape(t_p, 512), od.reshape(t_p, 1024), gates.reshape(t_p, 2048), xp.reshape(t_p, D),
                 mod_p, tm_post, tm_peer, te, s_len // tm_peer)
        for lst, r in zip(new_p, (ckv, kr, dk.reshape(nb, s_len, DIFF_KV_HEADS, 2 * DIFF_QK),
                                  dv.reshape(nb, s_len, DIFF_KV_HEADS, DIFF_V))):
            lst.append(r)

        (ckv_s, kr_s, dk_s, dv_s, q_s, _, _, _, dq_s, _, _, gates_s) = mixer(xs, mod_s, tab_s, n_seq)
        qlat = _small_mm(q_s[0].astype(F32), w_abs, tn=1024).reshape(n_seq, MLA_HEADS, KV_LORA + LANES)
        dq4 = dq_s[0].astype(F32).reshape(n_seq, DIFF_KV_HEADS, DIFF_GROUP, 2, DIFF_QK)
        zq = jnp.zeros_like(dq4[:, :, :, 0])
        qd = jnp.concatenate([jnp.concatenate([dq4[:, :, :, 0], zq], axis=-1),
                              jnp.concatenate([zq, dq4[:, :, :, 1]], axis=-1)],
                             axis=2).reshape(n_seq, DIFF_KV_HEADS * MLA_HEADS, LANES)
        olat, od_s = _decode_attention(
            page_table, lam, qlat, qd, ckv_s.reshape(n_seq, 1, KV_LORA), kr_s.reshape(n_seq, 1, MLA_ROPE),
            dk_s.reshape(n_seq, 1, 256), dv_s.reshape(n_seq, 1, 256), diff_subln_g[l],
            cache_ckv, cache_kr, cache_dk, cache_dv, l, lam_init, pps)
        om_s = _small_mm(olat.reshape(n_seq, MLA_HEADS * KV_LORA), w_uvbd).astype(BF16)
        ys = ffn(om_s, od_s.reshape(n_seq, 1024).astype(BF16), gates_s[0], xs[0], mod_s, n_seq, n_seq, n_seq, 1)
        for lst, r in zip(new_s, (ckv_s.reshape(n_seq, 1, KV_LORA), kr_s.reshape(n_seq, 1, MLA_ROPE),
                                  dk_s.reshape(n_seq, 1, DIFF_KV_HEADS, 2 * DIFF_QK),
                                  dv_s.reshape(n_seq, 1, DIFF_KV_HEADS, DIFF_V))):
            lst.append(r)
        if not last:
            raise NotImplementedError("multi-layer stacking needs the un-normalised residual stream")

    y_prompt = yp.reshape(nb, s_len, D)
    y_sample = ys.reshape(n_seq, 1, D)
    stack = lambda t: jnp.stack(t, axis=0)
    return (y_prompt, y_sample, *[stack(t) for t in new_p], *[stack(t) for t in new_s])
```
